```python
import jax, jax.numpy as jnp
from jax import lax
import numpy as np

D_MODEL = 2048
BATCH = 2
SEQ = 8192
DEPTH = 1

N_ATTN_HEADS = 8
HEAD_DIM = 128
ATTN_WIDTH = N_ATTN_HEADS * HEAD_DIM
CONV_GROUPS = 8
CONV_WIDTH = CONV_GROUPS * 128
CONV_K = 3
MOBA_BLOCK = 256
MOBA_TOPK = 3
Q_CHUNK = 32
ROPE_THETA = 500000.0
ROPE_DIM = HEAD_DIM // 4
N_MEM = 256
N_XATTN_HEADS = 4
XATTN_WIDTH = N_XATTN_HEADS * HEAD_DIM
D_FF = 4 * D_MODEL
NORM_EPS = 1e-6
IN_PROJ_WIDTH = 3 * ATTN_WIDTH + 3 * CONV_WIDTH + 2 * D_MODEL

kernel_name = "hybrid_moba_shortconv_gated_block"


def rms_norm(x, g):
    xf = x.astype(jnp.float32)
    y = xf * lax.rsqrt(jnp.mean(xf * xf, axis=-1, keepdims=True) + NORM_EPS)
    return (y * g.astype(jnp.float32)).astype(x.dtype)


def partial_rope(t, pos):
    half = ROPE_DIM // 2
    inv_freq = ROPE_THETA ** (-jnp.arange(half, dtype=jnp.float32) / half)
    ang = pos.astype(jnp.float32)[:, None] * inv_freq[None, :]
    cos, sin = jnp.cos(ang), jnp.sin(ang)
    tr = t[..., :ROPE_DIM].astype(jnp.float32)
    t1, t2 = tr[..., :half], tr[..., half:]
    rot = jnp.concatenate([t1 * cos - t2 * sin, t2 * cos + t1 * sin], axis=-1).astype(t.dtype)
    return jnp.concatenate([rot, t[..., ROPE_DIM:]], axis=-1)


def moba_attention(q, k, v):
    B, H, S, Dh = q.shape
    nb = -(-S // MOBA_BLOCK)
    s_pad = nb * MOBA_BLOCK
    n_ch = s_pad // Q_CHUNK
    padw = ((0, 0), (0, 0), (0, s_pad - S), (0, 0))
    q, k, v = (jnp.pad(t, padw) for t in (q, k, v))
    kb = k.reshape(B, H, nb, MOBA_BLOCK, Dh)
    vb = v.reshape(B, H, nb, MOBA_BLOCK, Dh)
    k_mean = jnp.mean(kb.astype(jnp.float32), axis=3)
    gate = jnp.einsum('bhsd,bhnd->bhsn', q.astype(jnp.float32), k_mean)
    q_blk = jnp.arange(s_pad) // MOBA_BLOCK
    fully_past = jnp.arange(nb)[None, :] < q_blk[:, None]
    gate = jnp.where(fully_past, gate, -jnp.inf)
    top_val, top_idx = lax.top_k(gate, min(MOBA_TOPK, nb))
    own = jnp.broadcast_to(q_blk[:, None], (B, H, s_pad, 1)).astype(top_idx.dtype)
    blk_idx = jnp.concatenate([top_idx, own], axis=-1)
    blk_ok = jnp.concatenate([jnp.isfinite(top_val), jnp.ones((B, H, s_pad, 1), dtype=bool)], axis=-1)

    def to_chunks(t):
        return jnp.moveaxis(t.reshape((B, H, n_ch, Q_CHUNK) + t.shape[3:]), 2, 0)

    qpos = jnp.arange(s_pad).reshape(n_ch, Q_CHUNK)
    offs = jnp.arange(MOBA_BLOCK)
    scale = HEAD_DIM ** -0.5
    gather = jax.vmap(jax.vmap(lambda tb, ib: tb[ib]))

    def attend_chunk(args):
        qc, idx, ok, qp = args
        kg = gather(kb, idx)
        vg = gather(vb, idx)
        s = jnp.einsum('bhqd,bhqnkd->bhqnk', qc, kg).astype(jnp.float32) * scale
        kpos = idx[..., None] * MOBA_BLOCK + offs
        mask = ok[..., None] & (kpos <= qp[:, None, None])
        s = jnp.where(mask, s, -jnp.inf)
        p = jax.nn.softmax(s.reshape(B, H, Q_CHUNK, -1), axis=-1).reshape(s.shape)
        return jnp.einsum('bhqnk,bhqnkd->bhqd', p.astype(vg.dtype), vg)

    out = lax.map(attend_chunk, (to_chunks(q), to_chunks(blk_idx), to_chunks(blk_ok), qpos))
    return jnp.moveaxis(out, 0, 2).reshape(B, H, s_pad, Dh)[:, :, :S]


def short_conv(u, w):
    S = u.shape[1]
    up = jnp.pad(u, ((0, 0), (CONV_K - 1, 0), (0, 0)))
    y = up[:, 0:S, :] * w[0]
    for j in range(1, CONV_K):
        y = y + up[:, j:j + S, :] * w[j]
    return y


def split_in_proj(proj):
    widths = [ATTN_WIDTH] * 3 + [CONV_WIDTH] * 3 + [D_MODEL, D_MODEL]
    cuts, acc = [], 0
    for w in widths[:-1]:
        acc += w
        cuts.append(acc)
    return jnp.split(proj, cuts, axis=-1)


def heads(t, n_heads):
    B, S, _ = t.shape
    return t.reshape(B, S, n_heads, HEAD_DIM).transpose(0, 2, 1, 3)


def merge_heads(t):
    B, H, S, Dh = t.shape
    return t.transpose(0, 2, 1, 3).reshape(B, S, H * Dh)


def hybrid_layer(x, mem, norm_mix, w_in, conv_w, w_attn_out, w_conv_out, w_mix_out,
                 norm_xattn, norm_mem, wq_x, wkv_x, wo_x, norm_mlp, w_up, w_down):
    B, S, _ = x.shape
    pos = jnp.arange(S)
    h = rms_norm(x, norm_mix)
    q, k, v, cx, cb, cc, ga, gc = split_in_proj(h @ w_in)
    q = partial_rope(heads(q, N_ATTN_HEADS), pos)
    k = partial_rope(heads(k, N_ATTN_HEADS), pos)
    y_attn = merge_heads(moba_attention(q, k, heads(v, N_ATTN_HEADS))) @ w_attn_out
    y_conv = (cb * short_conv(cc * cx, conv_w)) @ w_conv_out
    merged = jax.nn.sigmoid(ga) * y_attn + jax.nn.sigmoid(gc) * y_conv
    x = x + merged @ w_mix_out
    hq = heads(rms_norm(x, norm_xattn) @ wq_x, N_XATTN_HEADS)
    mk, mv = jnp.split(rms_norm(mem, norm_mem) @ wkv_x, 2, axis=-1)
    mk, mv = heads(mk, N_XATTN_HEADS), heads(mv, N_XATTN_HEADS)
    s = jnp.einsum('bhsd,bhmd->bhsm', hq, mk).astype(jnp.float32) * (HEAD_DIM ** -0.5)
    p = jax.nn.softmax(s, axis=-1).astype(mv.dtype)
    x = x + merge_heads(jnp.einsum('bhsm,bhmd->bhsd', p, mv)) @ wo_x
    u = jax.nn.relu(rms_norm(x, norm_mlp) @ w_up)
    return x + (u * u) @ w_down


def setup_inputs(seed: int = 0) -> dict:
    key = jax.random.key(seed)
    ks = jax.random.split(key, 20)

    def dense(k, shape, fan_in):
        return jax.random.normal(k, shape, jnp.float32) * (fan_in ** -0.5)

    def gain(k, shape):
        return 1.0 + 0.01 * jax.random.normal(k, shape, jnp.float32)

    L, D = DEPTH, D_MODEL
    return {
        "x": jax.random.normal(ks[0], (BATCH, SEQ, D), jnp.float32),
        "mem": jax.random.normal(ks[1], (BATCH, N_MEM, D), jnp.float32),
        "norm_mix": gain(ks[2], (L, D)),
        "w_in": dense(ks[3], (L, D, IN_PROJ_WIDTH), D),
        "conv_w": dense(ks[4], (L, CONV_K, CONV_WIDTH), CONV_K),
        "w_attn_out": dense(ks[5], (L, ATTN_WIDTH, D), ATTN_WIDTH),
        "w_conv_out": dense(ks[6], (L, CONV_WIDTH, D), CONV_WIDTH),
        "w_mix_out": dense(ks[7], (L, D, D), D),
        "norm_xattn": gain(ks[8], (L, D)),
        "norm_mem": gain(ks[9], (L, D)),
        "wq_x": dense(ks[10], (L, D, XATTN_WIDTH), D),
        "wkv_x": dense(ks[11], (L, D, 2 * XATTN_WIDTH), D),
        "wo_x": dense(ks[12], (L, XATTN_WIDTH, D), XATTN_WIDTH),
        "norm_mlp": gain(ks[13], (L, D)),
        "w_up": dense(ks[14], (L, D, D_FF), D),
        "w_down": dense(ks[15], (L, D_FF, D), D_FF),
        "norm_final": gain(ks[16], (D,)),
    }


def reference(x, mem, norm_mix, w_in, conv_w, w_attn_out, w_conv_out, w_mix_out,
              norm_xattn, norm_mem, wq_x, wkv_x, wo_x, norm_mlp, w_up, w_down, norm_final):
    for l in range(DEPTH):
        x = hybrid_layer(x, mem, norm_mix[l], w_in[l], conv_w[l], w_attn_out[l], w_conv_out[l],
                         w_mix_out[l], norm_xattn[l], norm_mem[l], wq_x[l], wkv_x[l], wo_x[l],
                         norm_mlp[l], w_up[l], w_down[l])
    return rms_norm(x, norm_final)
```

```python
import functools
import math

import jax
import jax.numpy as jnp
from jax import lax
from jax.experimental import pallas as pl
from jax.experimental.pallas import tpu as pltpu

F32 = jnp.float32
BF16 = jnp.bfloat16

HEAD_DIM = 128
N_ATTN_HEADS = 8
ATTN_WIDTH = N_ATTN_HEADS * HEAD_DIM
CONV_WIDTH = 1024
CONV_K = 3
MOBA_BLOCK = 256
MOBA_TOPK = 3
ROPE_THETA = 500000.0
ROPE_DIM = HEAD_DIM // 4
N_XATTN_HEADS = 4
XATTN_WIDTH = N_XATTN_HEADS * HEAD_DIM
NORM_EPS = 1e-6

LOG2E = math.log2(math.e)
ATTN_SCALE = HEAD_DIM ** -0.5
NEG_INF = float("-inf")

VMEM_LIMIT_BYTES = 52 * 1024 * 1024
SUBLANES = 8

NT_DIMS = (((1,), (1,)), ((), ()))
TN_DIMS = (((0,), (0,)), ((), ()))


def _rms_norm_bf16(xf, g):
    ms = jnp.mean(xf * xf, axis=-1, keepdims=True)
    return (xf * lax.rsqrt(ms + NORM_EPS) * g).astype(BF16)


IN_TM = 512
IN_TN = 1024


def _rope(t, cos, sin_lo, sin_hi):
    return (t * cos + pltpu.roll(t, HEAD_DIM - ROPE_DIM // 2, 1) * sin_lo
            + pltpu.roll(t, ROPE_DIM // 2, 1) * sin_hi)


def _in_proj_kernel(tiles_per_batch, x_ref, g_ref, w_ref, cos_ref, slo_ref, shi_ref, cw_ref,
                    q_ref, k_ref, v_ref, km_ref, z_ref, sga_ref, sgc_ref,
                    h_scr, cx_scr, u_scr):
    i = pl.program_id(0)
    j = pl.program_id(1)
    tm = IN_TM

    @pl.when(j == 0)
    def _():
        h_scr[...] = _rms_norm_bf16(x_ref[...], g_ref[...])

    acc = jnp.dot(h_scr[...], w_ref[...], preferred_element_type=F32)

    @pl.when(j == 0)
    def _():
        cos, slo, shi = cos_ref[...], slo_ref[...], shi_ref[...]
        for hh in range(N_ATTN_HEADS):
            sl = slice(hh * HEAD_DIM, (hh + 1) * HEAD_DIM)
            q_ref[:, sl] = (_rope(acc[:, sl], cos, slo, shi) * (ATTN_SCALE * LOG2E)).astype(BF16)

    @pl.when(j == 1)
    def _():
        cos, slo, shi = cos_ref[...], slo_ref[...], shi_ref[...]
        for hh in range(N_ATTN_HEADS):
            sl = slice(hh * HEAD_DIM, (hh + 1) * HEAD_DIM)
            kf = _rope(acc[:, sl], cos, slo, shi)
            k_ref[:, sl] = kf.astype(BF16)
            for r in range(tm // MOBA_BLOCK):
                km_ref[r, :, sl] = jnp.mean(kf[r * MOBA_BLOCK:(r + 1) * MOBA_BLOCK], axis=0, keepdims=True)

    @pl.when(j == 2)
    def _():
        v_ref[...] = acc.astype(BF16)

    @pl.when(j == 3)
    def _():
        cx_scr[...] = acc

    @pl.when(j == 4)
    def _():
        @pl.when(i % tiles_per_batch == 0)
        def _():
            u_scr[0:SUBLANES, :] = jnp.zeros((SUBLANES, IN_TN), F32)
        u_scr[SUBLANES:tm + SUBLANES, :] = acc * cx_scr[...]

    @pl.when(j == 5)
    def _():
        w = cw_ref[...]
        conv = w[0:1] * u_scr[SUBLANES - 2:tm + SUBLANES - 2, :]
        conv = conv + w[1:2] * u_scr[SUBLANES - 1:tm + SUBLANES - 1, :]
        conv = conv + w[2:3] * u_scr[SUBLANES:tm + SUBLANES, :]
        z_ref[...] = (acc * conv).astype(BF16)
        u_scr[0:SUBLANES, :] = u_scr[tm:tm + SUBLANES, :]

    @pl.when((j == 6) | (j == 7))
    def _():
        sga_ref[...] = jax.nn.sigmoid(acc).astype(BF16)

    @pl.when(j >= 8)
    def _():
        sgc_ref[...] = jax.nn.sigmoid(acc).astype(BF16)


def _in_proj(xf, g, w_in, cos, slo, shi, conv_w, seq):
    T, D = xf.shape
    tm, tn = IN_TM, IN_TN
    tpb = seq // tm
    n_col = w_in.shape[1] // tn

    def w_map(i, j):
        return (0, j + jnp.where(j == 4, 1, 0) - jnp.where(j == 5, 1, 0))

    row = lambda i, j: (i, 0)
    pos = lambda i, j: (i % tpb, 0)
    out_shape = (
        jax.ShapeDtypeStruct((T, ATTN_WIDTH), BF16),
        jax.ShapeDtypeStruct((T, ATTN_WIDTH), BF16),
        jax.ShapeDtypeStruct((T, ATTN_WIDTH), BF16),
        jax.ShapeDtypeStruct((T // MOBA_BLOCK, 1, ATTN_WIDTH), F32),
        jax.ShapeDtypeStruct((T, CONV_WIDTH), BF16),
        jax.ShapeDtypeStruct((T, D), BF16),
        jax.ShapeDtypeStruct((T, D), BF16),
    )
    return pl.pallas_call(
        functools.partial(_in_proj_kernel, tpb),
        grid=(T // tm, n_col),
        in_specs=[
            pl.BlockSpec((tm, D), row),
            pl.BlockSpec((1, D), lambda i, j: (0, 0)),
            pl.BlockSpec((D, tn), w_map),
            pl.BlockSpec((tm, HEAD_DIM), pos),
            pl.BlockSpec((tm, HEAD_DIM), pos),
            pl.BlockSpec((tm, HEAD_DIM), pos),
            pl.BlockSpec((CONV_K, CONV_WIDTH), lambda i, j: (0, 0)),
        ],
        out_specs=(
            pl.BlockSpec((tm, tn), row),
            pl.BlockSpec((tm, tn), row),
            pl.BlockSpec((tm, tn), row),
            pl.BlockSpec((tm // MOBA_BLOCK, 1, tn), lambda i, j: (i, 0, 0)),
            pl.BlockSpec((tm, tn), row),
            pl.BlockSpec((tm, tn), lambda i, j: (i, jnp.clip(j - 6, 0, 1))),
            pl.BlockSpec((tm, tn), lambda i, j: (i, jnp.clip(j - 8, 0, 1))),
        ),
        out_shape=out_shape,
        scratch_shapes=[
            pltpu.VMEM((tm, D), BF16),
            pltpu.VMEM((tm, tn), F32),
            pltpu.VMEM((tm + 2 * SUBLANES, tn), F32),
        ],
        compiler_params=pltpu.CompilerParams(
            dimension_semantics=("arbitrary", "arbitrary"),
            vmem_limit_bytes=VMEM_LIMIT_BYTES),
        name="in_proj",
    )(xf, g, w_in, cos, slo, shi, conv_w)


def _moba_kernel(nb, q_ref, k_ref, v_ref, km_ref, o_ref, bias_scr):
    qb = pl.program_id(2)
    blk = MOBA_BLOCK
    q = q_ref[...]

    km = km_ref[...]
    km_hi = km.astype(BF16)
    km_lo = (km - km_hi.astype(F32)).astype(BF16)
    gate = (lax.dot_general(km_hi, q, NT_DIMS, preferred_element_type=F32)
            + lax.dot_general(km_lo, q, NT_DIMS, preferred_element_type=F32))
    bidx = lax.broadcasted_iota(jnp.int32, (nb, blk), 0)
    g = jnp.where(bidx < qb, gate, NEG_INF)
    bias = jnp.full((nb, blk), NEG_INF, F32)
    for _ in range(MOBA_TOPK):
        top = jnp.max(g, axis=0, keepdims=True)
        first = jnp.min(jnp.where(g == top, bidx, nb), axis=0, keepdims=True)
        hit = bidx == first
        bias = jnp.where(hit & (top > NEG_INF), 0.0, bias)
        g = jnp.where(hit, NEG_INF, g)
    bias_scr[...] = bias

    start = pl.multiple_of(qb * blk, blk)
    s = lax.dot_general(k_ref[pl.ds(start, blk), :], q, NT_DIMS, preferred_element_type=F32)
    key_i = lax.broadcasted_iota(jnp.int32, (blk, blk), 0)
    qry_i = lax.broadcasted_iota(jnp.int32, (blk, blk), 1)
    s = jnp.where(key_i <= qry_i, s, NEG_INF)
    m = jnp.max(s, axis=0, keepdims=True)
    p = jnp.exp2(s - m)
    l = jnp.sum(p, axis=0, keepdims=True)
    acc = lax.dot_general(v_ref[pl.ds(start, blk), :], p.astype(BF16), TN_DIMS,
                          preferred_element_type=F32)

    def body(jb, carry):
        m, l, acc = carry
        off = pl.multiple_of(jb * blk, blk)
        s = lax.dot_general(k_ref[pl.ds(off, blk), :], q, NT_DIMS, preferred_element_type=F32)
        s = s + bias_scr[pl.ds(jb, 1), :]
        m_new = jnp.maximum(m, jnp.max(s, axis=0, keepdims=True))
        alpha = jnp.exp2(m - m_new)
        p = jnp.exp2(s - m_new)
        l = alpha * l + jnp.sum(p, axis=0, keepdims=True)
        pv = lax.dot_general(v_ref[pl.ds(off, blk), :], p.astype(BF16), TN_DIMS,
                             preferred_element_type=F32)
        return m_new, l, alpha * acc + pv

    m, l, acc = lax.fori_loop(0, qb, body, (m, l, acc))
    o_ref[...] = (acc / l).T.astype(BF16)


def _moba_attention(q, k, v, kmean, batch, seq):
    T = q.shape[0]
    nb = seq // MOBA_BLOCK
    return pl.pallas_call(
        functools.partial(_moba_kernel, nb),
        grid=(batch, N_ATTN_HEADS, nb),
        in_specs=[
            pl.BlockSpec((MOBA_BLOCK, HEAD_DIM), lambda b, h, i: (b * nb + i, h)),
            pl.BlockSpec((seq, HEAD_DIM), lambda b, h, i: (b, h)),
            pl.BlockSpec((seq, HEAD_DIM), lambda b, h, i: (b, h)),
            pl.BlockSpec((nb, HEAD_DIM), lambda b, h, i: (b, h)),
        ],
        out_specs=pl.BlockSpec((MOBA_BLOCK, HEAD_DIM), lambda b, h, i: (b * nb + i, h)),
        out_shape=jax.ShapeDtypeStruct((T, ATTN_WIDTH), BF16),
        scratch_shapes=[pltpu.VMEM((nb, MOBA_BLOCK), F32)],
        compiler_params=pltpu.CompilerParams(
            dimension_semantics=("arbitrary", "arbitrary", "arbitrary"),
            vmem_limit_bytes=VMEM_LIMIT_BYTES),
        name="moba_attention",
    )(q, k, v, kmean)


MIX_TM = 256


def _mix_kernel(a_ref, z_ref, sga_ref, sgc_ref, x_ref, wa_ref, wc_ref, wm_ref, o_ref):
    y_attn = jnp.dot(a_ref[...], wa_ref[...], preferred_element_type=F32)
    y_conv = jnp.dot(z_ref[...], wc_ref[...], preferred_element_type=F32)
    merged = sga_ref[...].astype(F32) * y_attn + sgc_ref[...].astype(F32) * y_conv
    o_ref[...] = x_ref[...] + jnp.dot(merged.astype(BF16), wm_ref[...], preferred_element_type=F32)


def _resident(shape):
    return pl.BlockSpec(shape, lambda *_: (0,) * len(shape), pipeline_mode=pl.Buffered(1))


def _mix(attn, z, sga, sgc, xf, w_attn_out, w_conv_out, w_mix_out):
    T, D = xf.shape
    tm = MIX_TM
    row = lambda i: (i, 0)
    return pl.pallas_call(
        _mix_kernel,
        grid=(T // tm,),
        in_specs=[
            pl.BlockSpec((tm, ATTN_WIDTH), row),
            pl.BlockSpec((tm, CONV_WIDTH), row),
            pl.BlockSpec((tm, D), row),
            pl.BlockSpec((tm, D), row),
            pl.BlockSpec((tm, D), row),
            _resident((ATTN_WIDTH, D)),
            _resident((CONV_WIDTH, D)),
            _resident((D, D)),
        ],
        out_specs=pl.BlockSpec((tm, D), row),
        out_shape=jax.ShapeDtypeStruct((T, D), F32),
        compiler_params=pltpu.CompilerParams(
            dimension_semantics=("arbitrary",), vmem_limit_bytes=VMEM_LIMIT_BYTES),
        name="mix",
    )(attn, z, sga, sgc, xf, w_attn_out, w_conv_out, w_mix_out)


def _mem_kv_kernel(mem_ref, g_ref, w_ref, o_ref):
    o_ref[...] = jnp.dot(_rms_norm_bf16(mem_ref[...], g_ref[...]), w_ref[...],
                         preferred_element_type=F32).astype(BF16)


def _mem_kv(memf, g, wkv):
    R, D = memf.shape
    N = wkv.shape[1]
    return pl.pallas_call(
        _mem_kv_kernel,
        grid=(1,),
        in_specs=[pl.BlockSpec((R, D), lambda i: (0, 0)),
                  pl.BlockSpec((1, D), lambda i: (0, 0)),
                  pl.BlockSpec((D, N), lambda i: (0, 0))],
        out_specs=pl.BlockSpec((R, N), lambda i: (0, 0)),
        out_shape=jax.ShapeDtypeStruct((R, N), BF16),
        compiler_params=pltpu.CompilerParams(vmem_limit_bytes=VMEM_LIMIT_BYTES),
        name="mem_kv",
    )(memf, g, wkv)


XATTN_TM = 512


def _xattn_kernel(x_ref, g_ref, wq_ref, kv_ref, wo_ref, o_ref):
    xf = x_ref[...]
    hq = jnp.dot(_rms_norm_bf16(xf, g_ref[...]), wq_ref[...], preferred_element_type=F32)
    hq = (hq * (ATTN_SCALE * LOG2E)).astype(BF16)
    outs = []
    for hh in range(N_XATTN_HEADS):
        sl = slice(hh * HEAD_DIM, (hh + 1) * HEAD_DIM)
        mk = kv_ref[:, sl]
        mv = kv_ref[:, XATTN_WIDTH + hh * HEAD_DIM:XATTN_WIDTH + (hh + 1) * HEAD_DIM]
        s = lax.dot_general(hq[:, sl], mk, NT_DIMS, preferred_element_type=F32)
        p = jnp.exp2(s - jnp.max(s, axis=-1, keepdims=True))
        inv = 1.0 / jnp.sum(p, axis=-1, keepdims=True)
        outs.append((jnp.dot(p.astype(BF16), mv, preferred_element_type=F32) * inv).astype(BF16))
    att = jnp.concatenate(outs, axis=-1)
    o_ref[...] = xf + jnp.dot(att, wo_ref[...], preferred_element_type=F32)


def _xattn(x1, g, wq, kv, wo, seq, n_mem):
    T, D = x1.shape
    tm = XATTN_TM
    tpb = seq // tm
    return pl.pallas_call(
        _xattn_kernel,
        grid=(T // tm,),
        in_specs=[
            pl.BlockSpec((tm, D), lambda i: (i, 0)),
            pl.BlockSpec((1, D), lambda i: (0, 0)),
            _resident((D, XATTN_WIDTH)),
            pl.BlockSpec((n_mem, 2 * XATTN_WIDTH), lambda i: (i // tpb, 0)),
            _resident((XATTN_WIDTH, D)),
        ],
        out_specs=pl.BlockSpec((tm, D), lambda i: (i, 0)),
        out_shape=jax.ShapeDtypeStruct((T, D), F32),
        compiler_params=pltpu.CompilerParams(
            dimension_semantics=("arbitrary",), vmem_limit_bytes=VMEM_LIMIT_BYTES),
        name="xattn",
    )(x1, g, wq, kv, wo)


MLP_TM = 512
MLP_TF = 512


def _mlp_kernel(x_ref, g_ref, wu_ref, wd_ref, gf_ref, o_ref, h_scr, acc_scr):
    f = pl.program_id(1)

    @pl.when(f == 0)
    def _():
        h_scr[...] = _rms_norm_bf16(x_ref[...], g_ref[...])

    u = jnp.maximum(jnp.dot(h_scr[...], wu_ref[...], preferred_element_type=F32), 0.0)
    part = jnp.dot((u * u).astype(BF16), wd_ref[...], preferred_element_type=F32)

    @pl.when(f == 0)
    def _():
        acc_scr[...] = x_ref[...] + part

    @pl.when(f > 0)
    def _():
        acc_scr[...] += part

    @pl.when(f == pl.num_programs(1) - 1)
    def _():
        y = acc_scr[...]
        ms = jnp.mean(y * y, axis=-1, keepdims=True)
        o_ref[...] = y * lax.rsqrt(ms + NORM_EPS) * gf_ref[...]


def _mlp(x2, g, w_up, w_down, g_final):
    T, D = x2.shape
    FF = w_up.shape[1]
    tm, tf = MLP_TM, MLP_TF
    return pl.pallas_call(
        _mlp_kernel,
        grid=(T // tm, FF // tf),
        in_specs=[
            pl.BlockSpec((tm, D), lambda i, f: (i, 0)),
            pl.BlockSpec((1, D), lambda i, f: (0, 0)),
            pl.BlockSpec((D, tf), lambda i, f: (0, f)),
            pl.BlockSpec((tf, D), lambda i, f: (f, 0)),
            pl.BlockSpec((1, D), lambda i, f: (0, 0)),
        ],
        out_specs=pl.BlockSpec((tm, D), lambda i, f: (i, 0)),
        out_shape=jax.ShapeDtypeStruct((T, D), F32),
        scratch_shapes=[pltpu.VMEM((tm, D), BF16), pltpu.VMEM((tm, D), F32)],
        compiler_params=pltpu.CompilerParams(
            dimension_semantics=("arbitrary", "arbitrary"), vmem_limit_bytes=VMEM_LIMIT_BYTES),
        name="mlp",
    )(x2, g, w_up, w_down, g_final)


def _rope_tables(seq):
    half = ROPE_DIM // 2
    inv_freq = ROPE_THETA ** (-jnp.arange(half, dtype=F32) / half)
    ang = jnp.arange(seq).astype(F32)[:, None] * inv_freq[None, :]
    cos, sin = jnp.cos(ang), jnp.sin(ang)
    pad = HEAD_DIM - ROPE_DIM
    cos_t = jnp.concatenate([cos, cos, jnp.ones((seq, pad), F32)], axis=-1)
    sin_lo = jnp.concatenate([-sin, jnp.zeros((seq, half + pad), F32)], axis=-1)
    sin_hi = jnp.concatenate([jnp.zeros((seq, half), F32), sin, jnp.zeros((seq, pad), F32)], axis=-1)
    return cos_t, sin_lo, sin_hi


def _layer(xf, memf, batch, seq, n_mem, norm_mix, w_in, conv_w, w_attn_out, w_conv_out, w_mix_out,
           norm_xattn, norm_mem, wq_x, wkv_x, wo_x, norm_mlp, w_up, w_down, g_out):
    T, D = xf.shape
    bf = lambda w: w.astype(BF16)
    row = lambda g: g.reshape(1, D).astype(F32)
    cos_t, sin_lo, sin_hi = _rope_tables(seq)
    q, k, v, kmean, z, sga, sgc = _in_proj(xf, row(norm_mix), bf(w_in), cos_t, sin_lo, sin_hi, conv_w, seq)
    kmean = kmean.reshape(T // MOBA_BLOCK, ATTN_WIDTH)
    attn = _moba_attention(q, k, v, kmean, batch, seq)
    x1 = _mix(attn, z, sga, sgc, xf, bf(w_attn_out), bf(w_conv_out), bf(w_mix_out))
    kv = _mem_kv(memf, row(norm_mem), bf(wkv_x))
    x2 = _xattn(x1, row(norm_xattn), bf(wq_x), kv, bf(wo_x), seq, n_mem)
    return _mlp(x2, row(norm_mlp), bf(w_up), bf(w_down), g_out)


def kernel(x, mem, norm_mix, w_in, conv_w, w_attn_out, w_conv_out, w_mix_out, norm_xattn, norm_mem,
           wq_x, wkv_x, wo_x, norm_mlp, w_up, w_down, norm_final):
    B, S, D = x.shape
    n_mem = mem.shape[1]
    depth = w_in.shape[0]
    assert depth == 1, "the final rmsnorm is fused into the last layer's MLP kernel"
    xf = x.reshape(B * S, D)
    memf = mem.reshape(B * n_mem, D)
    out = _layer(xf, memf, B, S, n_mem, norm_mix[0], w_in[0], conv_w[0], w_attn_out[0], w_conv_out[0],
                 w_mix_out[0], norm_xattn[0], norm_mem[0], wq_x[0], wkv_x[0], wo_x[0], norm_mlp[0],
                 w_up[0], w_down[0], norm_final.reshape(1, D).astype(F32))
    return out.reshape(B, S, D)
```

```python
import functools
import math

import jax
import jax.numpy as jnp
from jax import lax
from jax.experimental import pallas as pl
from jax.experimental.pallas import tpu as pltpu

F32 = jnp.float32
BF16 = jnp.bfloat16

HEAD_DIM = 128
N_ATTN_HEADS = 8
ATTN_WIDTH = N_ATTN_HEADS * HEAD_DIM
CONV_WIDTH = 1024
CONV_K = 3
MOBA_BLOCK = 256
MOBA_TOPK = 3
ROPE_THETA = 500000.0
ROPE_DIM = HEAD_DIM // 4
N_XATTN_HEADS = 4
XATTN_WIDTH = N_XATTN_HEADS * HEAD_DIM
NORM_EPS = 1e-6

LOG2E = math.log2(math.e)
ATTN_SCALE = HEAD_DIM ** -0.5
NEG_INF = float("-inf")

VMEM_LIMIT_BYTES = 52 * 1024 * 1024
SUBLANES = 8
MXU_COLS = 256

NT_DIMS = (((1,), (1,)), ((), ()))
TN_DIMS = (((0,), (0,)), ((), ()))


def _rms_norm_bf16(xf, g):
    ms = jnp.mean(xf * xf, axis=-1, keepdims=True)
    return (xf * lax.rsqrt(ms + NORM_EPS) * g).astype(BF16)


def _resident(shape):
    return pl.BlockSpec(shape, lambda *_: (0,) * len(shape), pipeline_mode=pl.Buffered(1))


NORM_TM = 1024


def _norm_kernel(x_ref, g_ref, o_ref):
    o_ref[...] = _rms_norm_bf16(x_ref[...], g_ref[...])


def _norm(xf, g):
    T, D = xf.shape
    tm = NORM_TM
    return pl.pallas_call(
        _norm_kernel,
        grid=(T // tm,),
        in_specs=[pl.BlockSpec((tm, D), lambda i: (i, 0)), pl.BlockSpec((1, D), lambda i: (0, 0))],
        out_specs=pl.BlockSpec((tm, D), lambda i: (i, 0)),
        out_shape=jax.ShapeDtypeStruct((T, D), BF16),
        compiler_params=pltpu.CompilerParams(
            dimension_semantics=("arbitrary",), vmem_limit_bytes=VMEM_LIMIT_BYTES),
        name="norm_mix",
    )(xf, g)


PROJ_TM = 512


def _rope(t, cos, sin_lo, sin_hi):
    return (t * cos + pltpu.roll(t, HEAD_DIM - ROPE_DIM // 2, 1) * sin_lo
            + pltpu.roll(t, ROPE_DIM // 2, 1) * sin_hi)


def _qkv_kernel(h_ref, w_ref, cos_ref, slo_ref, shi_ref, q_ref, k_ref, v_ref, km_ref):
    tm = PROJ_TM
    h = h_ref[...]
    cos, slo, shi = cos_ref[...], slo_ref[...], shi_ref[...]
    heads_per_chunk = MXU_COLS // HEAD_DIM
    for c in range(ATTN_WIDTH // MXU_COLS):
        csl = slice(c * MXU_COLS, (c + 1) * MXU_COLS)
        qc = jnp.dot(h, w_ref[:, csl], preferred_element_type=F32)
        kc = jnp.dot(h, w_ref[:, ATTN_WIDTH + c * MXU_COLS:ATTN_WIDTH + (c + 1) * MXU_COLS],
                     preferred_element_type=F32)
        vc = jnp.dot(h, w_ref[:, 2 * ATTN_WIDTH + c * MXU_COLS:2 * ATTN_WIDTH + (c + 1) * MXU_COLS],
                     preferred_element_type=F32)
        v_ref[:, csl] = vc.astype(BF16)
        for hh in range(heads_per_chunk):
            sl = slice(hh * HEAD_DIM, (hh + 1) * HEAD_DIM)
            osl = slice(c * MXU_COLS + hh * HEAD_DIM, c * MXU_COLS + (hh + 1) * HEAD_DIM)
            q_ref[:, osl] = (_rope(qc[:, sl], cos, slo, shi) * (ATTN_SCALE * LOG2E)).astype(BF16)
            kf = _rope(kc[:, sl], cos, slo, shi)
            k_ref[:, osl] = kf.astype(BF16)
            for r in range(tm // MOBA_BLOCK):
                km_ref[r, :, osl] = jnp.mean(kf[r * MOBA_BLOCK:(r + 1) * MOBA_BLOCK], axis=0, keepdims=True)


def _qkv_proj(h, w_qkv, cos, slo, shi, seq):
    T, D = h.shape
    tm = PROJ_TM
    tpb = seq // tm
    row = lambda i: (i, 0)
    pos = lambda i: (i % tpb, 0)
    return pl.pallas_call(
        _qkv_kernel,
        grid=(T // tm,),
        in_specs=[
            pl.BlockSpec((tm, D), row),
            _resident((D, 3 * ATTN_WIDTH)),
            pl.BlockSpec((tm, HEAD_DIM), pos),
            pl.BlockSpec((tm, HEAD_DIM), pos),
            pl.BlockSpec((tm, HEAD_DIM), pos),
        ],
        out_specs=(
            pl.BlockSpec((tm, ATTN_WIDTH), row),
            pl.BlockSpec((tm, ATTN_WIDTH), row),
            pl.BlockSpec((tm, ATTN_WIDTH), row),
            pl.BlockSpec((tm // MOBA_BLOCK, 1, ATTN_WIDTH), lambda i: (i, 0, 0)),
        ),
        out_shape=(
            jax.ShapeDtypeStruct((T, ATTN_WIDTH), BF16),
            jax.ShapeDtypeStruct((T, ATTN_WIDTH), BF16),
            jax.ShapeDtypeStruct((T, ATTN_WIDTH), BF16),
            jax.ShapeDtypeStruct((T // MOBA_BLOCK, 1, ATTN_WIDTH), F32),
        ),
        compiler_params=pltpu.CompilerParams(
            dimension_semantics=("arbitrary",), vmem_limit_bytes=VMEM_LIMIT_BYTES),
        name="qkv_proj",
    )(h, w_qkv, cos, slo, shi)


def _conv_kernel(tiles_per_batch, h_ref, w_ref, cw_ref, z_ref, u_scr):
    tm = PROJ_TM
    i = pl.program_id(0)

    @pl.when(i % tiles_per_batch == 0)
    def _():
        u_scr[0:SUBLANES, :] = jnp.zeros((SUBLANES, CONV_WIDTH), F32)

    h = h_ref[...]
    for c in range(CONV_WIDTH // MXU_COLS):
        csl = slice(c * MXU_COLS, (c + 1) * MXU_COLS)
        cx = jnp.dot(h, w_ref[:, csl], preferred_element_type=F32)
        cb = jnp.dot(h, w_ref[:, CONV_WIDTH + c * MXU_COLS:CONV_WIDTH + (c + 1) * MXU_COLS],
                     preferred_element_type=F32)
        cc = jnp.dot(h, w_ref[:, 2 * CONV_WIDTH + c * MXU_COLS:2 * CONV_WIDTH + (c + 1) * MXU_COLS],
                     preferred_element_type=F32)
        u = cc * cx
        u_scr[SUBLANES:tm + SUBLANES, csl] = u
        conv = (cw_ref[0:1, csl] * u_scr[SUBLANES - 2:tm + SUBLANES - 2, csl]
                + cw_ref[1:2, csl] * u_scr[SUBLANES - 1:tm + SUBLANES - 1, csl]
                + cw_ref[2:3, csl] * u)
        z_ref[:, csl] = (cb * conv).astype(BF16)
        u_scr[0:SUBLANES, csl] = u[tm - SUBLANES:tm]


def _conv_proj(h, w_conv, conv_w, seq):
    T, D = h.shape
    tm = PROJ_TM
    return pl.pallas_call(
        functools.partial(_conv_kernel, seq // tm),
        grid=(T // tm,),
        in_specs=[
            pl.BlockSpec((tm, D), lambda i: (i, 0)),
            _resident((D, 3 * CONV_WIDTH)),
            pl.BlockSpec((CONV_K, CONV_WIDTH), lambda i: (0, 0)),
        ],
        out_specs=pl.BlockSpec((tm, CONV_WIDTH), lambda i: (i, 0)),
        out_shape=jax.ShapeDtypeStruct((T, CONV_WIDTH), BF16),
        scratch_shapes=[pltpu.VMEM((tm + SUBLANES, CONV_WIDTH), F32)],
        compiler_params=pltpu.CompilerParams(
            dimension_semantics=("arbitrary",), vmem_limit_bytes=VMEM_LIMIT_BYTES),
        name="conv_proj",
    )(h, w_conv, conv_w)


def _gate_kernel(h_ref, w_ref, o_ref):
    h = h_ref[...]
    for c in range(o_ref.shape[1] // MXU_COLS):
        csl = slice(c * MXU_COLS, (c + 1) * MXU_COLS)
        o_ref[:, csl] = jax.nn.sigmoid(jnp.dot(h, w_ref[:, csl], preferred_element_type=F32)).astype(BF16)


def _gate_proj(h, w_gate):
    T, D = h.shape
    N = w_gate.shape[1]
    tm = PROJ_TM
    return pl.pallas_call(
        _gate_kernel,
        grid=(T // tm,),
        in_specs=[pl.BlockSpec((tm, D), lambda i: (i, 0)), _resident((D, N))],
        out_specs=pl.BlockSpec((tm, N), lambda i: (i, 0)),
        out_shape=jax.ShapeDtypeStruct((T, N), BF16),
        compiler_params=pltpu.CompilerParams(
            dimension_semantics=("arbitrary",), vmem_limit_bytes=VMEM_LIMIT_BYTES),
        name="gate_proj",
    )(h, w_gate)


MOBA_HEADS_PER_STEP = 2
MOBA_BLOCKS_PER_ITER = 2


def _moba_kernel(nb, q_ref, k_ref, v_ref, km_ref, o_ref, bias_scr):
    qb = pl.program_id(2)
    blk = MOBA_BLOCK
    n_heads = MOBA_HEADS_PER_STEP
    span = MOBA_BLOCKS_PER_ITER * blk
    start = pl.multiple_of(qb * blk, blk)
    bidx = lax.broadcasted_iota(jnp.int32, (nb, blk), 0)
    key_i = lax.broadcasted_iota(jnp.int32, (blk, blk), 0)
    qry_i = lax.broadcasted_iota(jnp.int32, (blk, blk), 1)

    qs, state = [], []
    for hh in range(n_heads):
        sl = slice(hh * HEAD_DIM, (hh + 1) * HEAD_DIM)
        q = q_ref[:, sl]
        qs.append(q)

        km = km_ref[:, sl]
        km_hi = km.astype(BF16)
        km_lo = (km - km_hi.astype(F32)).astype(BF16)
        gate = (lax.dot_general(km_hi, q, NT_DIMS, preferred_element_type=F32)
                + lax.dot_general(km_lo, q, NT_DIMS, preferred_element_type=F32))
        g = jnp.where(bidx < qb, gate, NEG_INF)
        bias = jnp.full((nb, blk), NEG_INF, F32)
        for _ in range(MOBA_TOPK):
            top = jnp.max(g, axis=0, keepdims=True)
            first = jnp.min(jnp.where(g == top, bidx, nb), axis=0, keepdims=True)
            hit = bidx == first
            bias = jnp.where(hit & (top > NEG_INF), 0.0, bias)
            g = jnp.where(hit, NEG_INF, g)
        bias_scr[hh] = bias

        s = lax.dot_general(k_ref[pl.ds(start, blk), sl], q, NT_DIMS, preferred_element_type=F32)
        s = jnp.where(key_i <= qry_i, s, NEG_INF)
        m = jnp.max(s, axis=0, keepdims=True)
        p = jnp.exp2(s - m)
        l = jnp.sum(p, axis=0, keepdims=True)
        acc = lax.dot_general(v_ref[pl.ds(start, blk), sl], p.astype(BF16), TN_DIMS,
                              preferred_element_type=F32)
        state.append((m, l, acc))

    def body(it, carry):
        off = pl.multiple_of(it * span, span)
        new = []
        for hh in range(n_heads):
            sl = slice(hh * HEAD_DIM, (hh + 1) * HEAD_DIM)
            m, l, acc = carry[hh]
            s = lax.dot_general(k_ref[pl.ds(off, span), sl], qs[hh], NT_DIMS, preferred_element_type=F32)
            s = jnp.concatenate(
                [s[u * blk:(u + 1) * blk] + bias_scr[hh, pl.ds(it * MOBA_BLOCKS_PER_ITER + u, 1), :]
                 for u in range(MOBA_BLOCKS_PER_ITER)], axis=0)
            m_new = jnp.maximum(m, jnp.max(s, axis=0, keepdims=True))
            alpha = jnp.exp2(m - m_new)
            p = jnp.exp2(s - m_new)
            l = alpha * l + jnp.sum(p, axis=0, keepdims=True)
            pv = lax.dot_general(v_ref[pl.ds(off, span), sl], p.astype(BF16), TN_DIMS,
                                 preferred_element_type=F32)
            new.append((m_new, l, alpha * acc + pv))
        return tuple(new)

    n_iter = (qb + MOBA_BLOCKS_PER_ITER - 1) // MOBA_BLOCKS_PER_ITER
    state = lax.fori_loop(0, n_iter, body, tuple(state))
    for hh in range(n_heads):
        m, l, acc = state[hh]
        o_ref[:, hh * HEAD_DIM:(hh + 1) * HEAD_DIM] = (acc / l).T.astype(BF16)


def _moba_attention(q, k, v, kmean, batch, seq):
    T = q.shape[0]
    nb = seq // MOBA_BLOCK
    assert nb % MOBA_BLOCKS_PER_ITER == 0 and N_ATTN_HEADS % MOBA_HEADS_PER_STEP == 0
    width = MOBA_HEADS_PER_STEP * HEAD_DIM
    return pl.pallas_call(
        functools.partial(_moba_kernel, nb),
        grid=(batch, N_ATTN_HEADS // MOBA_HEADS_PER_STEP, nb),
        in_specs=[
            pl.BlockSpec((MOBA_BLOCK, width), lambda b, h, i: (b * nb + i, h)),
            pl.BlockSpec((seq, width), lambda b, h, i: (b, h)),
            pl.BlockSpec((seq, width), lambda b, h, i: (b, h)),
            pl.BlockSpec((nb, width), lambda b, h, i: (b, h)),
        ],
        out_specs=pl.BlockSpec((MOBA_BLOCK, width), lambda b, h, i: (b * nb + i, h)),
        out_shape=jax.ShapeDtypeStruct((T, ATTN_WIDTH), BF16),
        scratch_shapes=[pltpu.VMEM((MOBA_HEADS_PER_STEP, nb, MOBA_BLOCK), F32)],
        compiler_params=pltpu.CompilerParams(
            dimension_semantics=("arbitrary", "arbitrary", "arbitrary"),
            vmem_limit_bytes=VMEM_LIMIT_BYTES),
        name="moba_attention",
    )(q, k, v, kmean)


MIX_TM = 256


def _mix_kernel(a_ref, z_ref, sga_ref, sgc_ref, x_ref, wa_ref, wc_ref, wm_ref, o_ref):
    y_attn = jnp.dot(a_ref[...], wa_ref[...], preferred_element_type=F32)
    y_conv = jnp.dot(z_ref[...], wc_ref[...], preferred_element_type=F32)
    merged = sga_ref[...].astype(F32) * y_attn + sgc_ref[...].astype(F32) * y_conv
    o_ref[...] = x_ref[...] + jnp.dot(merged.astype(BF16), wm_ref[...], preferred_element_type=F32)


def _mix(attn, z, gates, xf, w_attn_out, w_conv_out, w_mix_out):
    T, D = xf.shape
    tm = MIX_TM
    row = lambda i: (i, 0)
    return pl.pallas_call(
        _mix_kernel,
        grid=(T // tm,),
        in_specs=[
            pl.BlockSpec((tm, ATTN_WIDTH), row),
            pl.BlockSpec((tm, CONV_WIDTH), row),
            pl.BlockSpec((tm, D), row),
            pl.BlockSpec((tm, D), lambda i: (i, 1)),
            pl.BlockSpec((tm, D), row),
            _resident((ATTN_WIDTH, D)),
            _resident((CONV_WIDTH, D)),
            _resident((D, D)),
        ],
        out_specs=pl.BlockSpec((tm, D), row),
        out_shape=jax.ShapeDtypeStruct((T, D), F32),
        compiler_params=pltpu.CompilerParams(
            dimension_semantics=("arbitrary",), vmem_limit_bytes=VMEM_LIMIT_BYTES),
        name="mix",
    )(attn, z, gates, gates, xf, w_attn_out, w_conv_out, w_mix_out)


def _mem_kv_kernel(mem_ref, g_ref, w_ref, o_ref):
    o_ref[...] = jnp.dot(_rms_norm_bf16(mem_ref[...], g_ref[...]), w_ref[...],
                         preferred_element_type=F32).astype(BF16)


def _mem_kv(memf, g, wkv):
    R, D = memf.shape
    N = wkv.shape[1]
    return pl.pallas_call(
        _mem_kv_kernel,
        grid=(1,),
        in_specs=[pl.BlockSpec((R, D), lambda i: (0, 0)),
                  pl.BlockSpec((1, D), lambda i: (0, 0)),
                  pl.BlockSpec((D, N), lambda i: (0, 0))],
        out_specs=pl.BlockSpec((R, N), lambda i: (0, 0)),
        out_shape=jax.ShapeDtypeStruct((R, N), BF16),
        compiler_params=pltpu.CompilerParams(vmem_limit_bytes=VMEM_LIMIT_BYTES),
        name="mem_kv",
    )(memf, g, wkv)


XATTN_TM = 512


def _xattn_kernel(x_ref, g_ref, wq_ref, kv_ref, wo_ref, o_ref):
    xf = x_ref[...]
    hq = jnp.dot(_rms_norm_bf16(xf, g_ref[...]), wq_ref[...], preferred_element_type=F32)
    hq = (hq * (ATTN_SCALE * LOG2E)).astype(BF16)
    outs = []
    for hh in range(N_XATTN_HEADS):
        sl = slice(hh * HEAD_DIM, (hh + 1) * HEAD_DIM)
        mk = kv_ref[:, sl]
        mv = kv_ref[:, XATTN_WIDTH + hh * HEAD_DIM:XATTN_WIDTH + (hh + 1) * HEAD_DIM]
        s = lax.dot_general(hq[:, sl], mk, NT_DIMS, preferred_element_type=F32)
        p = jnp.exp2(s - jnp.max(s, axis=-1, keepdims=True))
        inv = 1.0 / jnp.sum(p, axis=-1, keepdims=True)
        outs.append((jnp.dot(p.astype(BF16), mv, preferred_element_type=F32) * inv).astype(BF16))
    att = jnp.concatenate(outs, axis=-1)
    o_ref[...] = xf + jnp.dot(att, wo_ref[...], preferred_element_type=F32)


def _xattn(x1, g, wq, kv, wo, seq, n_mem):
    T, D = x1.shape
    tm = XATTN_TM
    tpb = seq // tm
    return pl.pallas_call(
        _xattn_kernel,
        grid=(T // tm,),
        in_specs=[
            pl.BlockSpec((tm, D), lambda i: (i, 0)),
            pl.BlockSpec((1, D), lambda i: (0, 0)),
            _resident((D, XATTN_WIDTH)),
            pl.BlockSpec((n_mem, 2 * XATTN_WIDTH), lambda i: (i // tpb, 0)),
            _resident((XATTN_WIDTH, D)),
        ],
        out_specs=pl.BlockSpec((tm, D), lambda i: (i, 0)),
        out_shape=jax.ShapeDtypeStruct((T, D), F32),
        compiler_params=pltpu.CompilerParams(
            dimension_semantics=("arbitrary",), vmem_limit_bytes=VMEM_LIMIT_BYTES),
        name="xattn",
    )(x1, g, wq, kv, wo)


MLP_TM = 512
MLP_TF = 1024


def _mlp_kernel(x_ref, g_ref, wu_ref, wd_ref, gf_ref, o_ref, h_scr, acc_scr):
    f = pl.program_id(1)

    @pl.when(f == 0)
    def _():
        xf = x_ref[...]
        h_scr[...] = _rms_norm_bf16(xf, g_ref[...])
        acc_scr[...] = xf

    u = jnp.maximum(jnp.dot(h_scr[...], wu_ref[...], preferred_element_type=F32), 0.0)
    acc_scr[...] += jnp.dot((u * u).astype(BF16), wd_ref[...], preferred_element_type=F32)

    @pl.when(f == pl.num_programs(1) - 1)
    def _():
        y = acc_scr[...]
        ms = jnp.mean(y * y, axis=-1, keepdims=True)
        o_ref[...] = y * lax.rsqrt(ms + NORM_EPS) * gf_ref[...]


def _mlp(x2, g, w_up, w_down, g_final):
    T, D = x2.shape
    FF = w_up.shape[1]
    tm, tf = MLP_TM, MLP_TF
    return pl.pallas_call(
        _mlp_kernel,
        grid=(T // tm, FF // tf),
        in_specs=[
            pl.BlockSpec((tm, D), lambda i, f: (i, 0)),
            pl.BlockSpec((1, D), lambda i, f: (0, 0)),
            pl.BlockSpec((D, tf), lambda i, f: (0, f)),
            pl.BlockSpec((tf, D), lambda i, f: (f, 0)),
            pl.BlockSpec((1, D), lambda i, f: (0, 0)),
        ],
        out_specs=pl.BlockSpec((tm, D), lambda i, f: (i, 0)),
        out_shape=jax.ShapeDtypeStruct((T, D), F32),
        scratch_shapes=[pltpu.VMEM((tm, D), BF16), pltpu.VMEM((tm, D), F32)],
        compiler_params=pltpu.CompilerParams(
            dimension_semantics=("arbitrary", "arbitrary"), vmem_limit_bytes=VMEM_LIMIT_BYTES),
        name="mlp",
    )(x2, g, w_up, w_down, g_final)


def _rope_tables(seq):
    half = ROPE_DIM // 2
    inv_freq = ROPE_THETA ** (-jnp.arange(half, dtype=F32) / half)
    ang = jnp.arange(seq).astype(F32)[:, None] * inv_freq[None, :]
    cos, sin = jnp.cos(ang), jnp.sin(ang)
    pad = HEAD_DIM - ROPE_DIM
    cos_t = jnp.concatenate([cos, cos, jnp.ones((seq, pad), F32)], axis=-1)
    sin_lo = jnp.concatenate([-sin, jnp.zeros((seq, half + pad), F32)], axis=-1)
    sin_hi = jnp.concatenate([jnp.zeros((seq, half), F32), sin, jnp.zeros((seq, pad), F32)], axis=-1)
    return cos_t, sin_lo, sin_hi


def _layer(xf, memf, batch, seq, n_mem, norm_mix, w_in, conv_w, w_attn_out, w_conv_out, w_mix_out,
           norm_xattn, norm_mem, wq_x, wkv_x, wo_x, norm_mlp, w_up, w_down, g_out):
    T, D = xf.shape
    bf = lambda w: w.astype(BF16)
    row = lambda g: g.reshape(1, D).astype(F32)
    cos_t, sin_lo, sin_hi = _rope_tables(seq)
    n_qkv, n_conv = 3 * ATTN_WIDTH, 3 * CONV_WIDTH
    h = _norm(xf, row(norm_mix))
    q, k, v, kmean = _qkv_proj(h, bf(w_in[:, :n_qkv]), cos_t, sin_lo, sin_hi, seq)
    z = _conv_proj(h, bf(w_in[:, n_qkv:n_qkv + n_conv]), conv_w, seq)
    gates = _gate_proj(h, bf(w_in[:, n_qkv + n_conv:]))
    kmean = kmean.reshape(T // MOBA_BLOCK, ATTN_WIDTH)
    attn = _moba_attention(q, k, v, kmean, batch, seq)
    x1 = _mix(attn, z, gates, xf, bf(w_attn_out), bf(w_conv_out), bf(w_mix_out))
    kv = _mem_kv(memf, row(norm_mem), bf(wkv_x))
    x2 = _xattn(x1, row(norm_xattn), bf(wq_x), kv, bf(wo_x), seq, n_mem)
    return _mlp(x2, row(norm_mlp), bf(w_up), bf(w_down), g_out)


def kernel(x, mem, norm_mix, w_in, conv_w, w_attn_out, w_conv_out, w_mix_out, norm_xattn, norm_mem,
           wq_x, wkv_x, wo_x, norm_mlp, w_up, w_down, norm_final):
    B, S, D = x.shape
    n_mem = mem.shape[1]
    depth = w_in.shape[0]
    assert depth == 1, "the final rmsnorm is fused into the last layer's MLP kernel"
    xf = x.reshape(B * S, D)
    memf = mem.reshape(B * n_mem, D)
    out = _layer(xf, memf, B, S, n_mem, norm_mix[0], w_in[0], conv_w[0], w_attn_out[0], w_conv_out[0],
                 w_mix_out[0], norm_xattn[0], norm_mem[0], wq_x[0], wkv_x[0], wo_x[0], norm_mlp[0],
                 w_up[0], w_down[0], norm_final.reshape(1, D).astype(F32))
    return out.reshape(B, S, D)
```

```python
import functools
import math

import jax
import jax.numpy as jnp
import numpy as np
from jax import lax
from jax.experimental import pallas as pl
from jax.experimental.pallas import tpu as pltpu

F32 = jnp.float32
BF16 = jnp.bfloat16

HEAD_DIM = 128
N_ATTN_HEADS = 8
ATTN_WIDTH = N_ATTN_HEADS * HEAD_DIM
CONV_WIDTH = 1024
CONV_K = 3
MOBA_BLOCK = 256
MOBA_TOPK = 3
ROPE_THETA = 500000.0
ROPE_DIM = HEAD_DIM // 4
N_XATTN_HEADS = 4
XATTN_WIDTH = N_XATTN_HEADS * HEAD_DIM
NORM_EPS = 1e-6

LOG2E = math.log2(math.e)
ATTN_SCALE = HEAD_DIM ** -0.5
NEG_INF = float("-inf")

VMEM_LIMIT_BYTES = 52 * 1024 * 1024
SUBLANES = 8
MXU_COLS = 256

NT_DIMS = (((1,), (1,)), ((), ()))
TN_DIMS = (((0,), (0,)), ((), ()))


def _rms_norm_bf16(xf, g):
    ms = jnp.mean(xf * xf, axis=-1, keepdims=True)
    return (xf * lax.rsqrt(ms + NORM_EPS) * g).astype(BF16)


def _resident(shape):
    return pl.BlockSpec(shape, lambda *_: (0,) * len(shape), pipeline_mode=pl.Buffered(1))


NORM_TM = 1024


def _norm_kernel(x_ref, g_ref, o_ref):
    o_ref[...] = _rms_norm_bf16(x_ref[...], g_ref[...])


def _norm(xf, g):
    T, D = xf.shape
    tm = NORM_TM
    return pl.pallas_call(
        _norm_kernel,
        grid=(T // tm,),
        in_specs=[pl.BlockSpec((tm, D), lambda i: (i, 0)), pl.BlockSpec((1, D), lambda i: (0, 0))],
        out_specs=pl.BlockSpec((tm, D), lambda i: (i, 0)),
        out_shape=jax.ShapeDtypeStruct((T, D), BF16),
        compiler_params=pltpu.CompilerParams(
            dimension_semantics=("arbitrary",), vmem_limit_bytes=VMEM_LIMIT_BYTES),
        name="norm_mix",
    )(xf, g)


PROJ_TM = 512


def _rope(t, cos, sin_lo, sin_hi):
    return (t * cos + pltpu.roll(t, HEAD_DIM - ROPE_DIM // 2, 1) * sin_lo
            + pltpu.roll(t, ROPE_DIM // 2, 1) * sin_hi)


def _qkv_kernel(h_ref, w_ref, cos_ref, slo_ref, shi_ref, q_ref, k_ref, vt_ref, km_ref):
    tm = PROJ_TM
    h = h_ref[...]
    cos, slo, shi = cos_ref[...], slo_ref[...], shi_ref[...]
    heads_per_chunk = MXU_COLS // HEAD_DIM
    for c in range(ATTN_WIDTH // MXU_COLS):
        csl = slice(c * MXU_COLS, (c + 1) * MXU_COLS)
        qc = jnp.dot(h, w_ref[:, csl], preferred_element_type=F32)
        kc = jnp.dot(h, w_ref[:, ATTN_WIDTH + c * MXU_COLS:ATTN_WIDTH + (c + 1) * MXU_COLS],
                     preferred_element_type=F32)
        vc = jnp.dot(h, w_ref[:, 2 * ATTN_WIDTH + c * MXU_COLS:2 * ATTN_WIDTH + (c + 1) * MXU_COLS],
                     preferred_element_type=F32)
        vt_ref[0, csl, :] = vc.T.astype(BF16)
        for hh in range(heads_per_chunk):
            sl = slice(hh * HEAD_DIM, (hh + 1) * HEAD_DIM)
            osl = slice(c * MXU_COLS + hh * HEAD_DIM, c * MXU_COLS + (hh + 1) * HEAD_DIM)
            q_ref[:, osl] = (_rope(qc[:, sl], cos, slo, shi) * (ATTN_SCALE * LOG2E)).astype(BF16)
            kf = _rope(kc[:, sl], cos, slo, shi)
            k_ref[:, osl] = kf.astype(BF16)
            for r in range(tm // MOBA_BLOCK):
                km_ref[r, :, osl] = jnp.mean(kf[r * MOBA_BLOCK:(r + 1) * MOBA_BLOCK], axis=0, keepdims=True)


def _qkv_proj(h, w_qkv, cos, slo, shi, seq):
    T, D = h.shape
    tm = PROJ_TM
    tpb = seq // tm
    row = lambda i: (i, 0)
    pos = lambda i: (i % tpb, 0)
    return pl.pallas_call(
        _qkv_kernel,
        grid=(T // tm,),
        in_specs=[
            pl.BlockSpec((tm, D), row),
            _resident((D, 3 * ATTN_WIDTH)),
            pl.BlockSpec((tm, HEAD_DIM), pos),
            pl.BlockSpec((tm, HEAD_DIM), pos),
            pl.BlockSpec((tm, HEAD_DIM), pos),
        ],
        out_specs=(
            pl.BlockSpec((tm, ATTN_WIDTH), row),
            pl.BlockSpec((tm, ATTN_WIDTH), row),
            pl.BlockSpec((1, ATTN_WIDTH, tm), lambda i: (i, 0, 0)),
            pl.BlockSpec((tm // MOBA_BLOCK, 1, ATTN_WIDTH), lambda i: (i, 0, 0)),
        ),
        out_shape=(
            jax.ShapeDtypeStruct((T, ATTN_WIDTH), BF16),
            jax.ShapeDtypeStruct((T, ATTN_WIDTH), BF16),
            jax.ShapeDtypeStruct((T // tm, ATTN_WIDTH, tm), BF16),
            jax.ShapeDtypeStruct((T // MOBA_BLOCK, 1, ATTN_WIDTH), F32),
        ),
        compiler_params=pltpu.CompilerParams(
            dimension_semantics=("arbitrary",), vmem_limit_bytes=VMEM_LIMIT_BYTES),
        name="qkv_proj",
    )(h, w_qkv, cos, slo, shi)


def _conv_kernel(tiles_per_batch, h_ref, w_ref, cw_ref, z_ref, u_scr):
    tm = PROJ_TM
    i = pl.program_id(0)

    @pl.when(i % tiles_per_batch == 0)
    def _():
        u_scr[0:SUBLANES, :] = jnp.zeros((SUBLANES, CONV_WIDTH), F32)

    h = h_ref[...]
    for c in range(CONV_WIDTH // MXU_COLS):
        csl = slice(c * MXU_COLS, (c + 1) * MXU_COLS)
        cx = jnp.dot(h, w_ref[:, csl], preferred_element_type=F32)
        cb = jnp.dot(h, w_ref[:, CONV_WIDTH + c * MXU_COLS:CONV_WIDTH + (c + 1) * MXU_COLS],
                     preferred_element_type=F32)
        cc = jnp.dot(h, w_ref[:, 2 * CONV_WIDTH + c * MXU_COLS:2 * CONV_WIDTH + (c + 1) * MXU_COLS],
                     preferred_element_type=F32)
        u = cc * cx
        u_scr[SUBLANES:tm + SUBLANES, csl] = u
        conv = (cw_ref[0:1, csl] * u_scr[SUBLANES - 2:tm + SUBLANES - 2, csl]
                + cw_ref[1:2, csl] * u_scr[SUBLANES - 1:tm + SUBLANES - 1, csl]
                + cw_ref[2:3, csl] * u)
        z_ref[:, csl] = (cb * conv).astype(BF16)
        u_scr[0:SUBLANES, csl] = u[tm - SUBLANES:tm]


def _conv_proj(h, w_conv, conv_w, seq):
    T, D = h.shape
    tm = PROJ_TM
    return pl.pallas_call(
        functools.partial(_conv_kernel, seq // tm),
        grid=(T // tm,),
        in_specs=[
            pl.BlockSpec((tm, D), lambda i: (i, 0)),
            _resident((D, 3 * CONV_WIDTH)),
            pl.BlockSpec((CONV_K, CONV_WIDTH), lambda i: (0, 0)),
        ],
        out_specs=pl.BlockSpec((tm, CONV_WIDTH), lambda i: (i, 0)),
        out_shape=jax.ShapeDtypeStruct((T, CONV_WIDTH), BF16),
        scratch_shapes=[pltpu.VMEM((tm + SUBLANES, CONV_WIDTH), F32)],
        compiler_params=pltpu.CompilerParams(
            dimension_semantics=("arbitrary",), vmem_limit_bytes=VMEM_LIMIT_BYTES),
        name="conv_proj",
    )(h, w_conv, conv_w)


def _gate_kernel(h_ref, w_ref, o_ref):
    h = h_ref[...]
    for c in range(o_ref.shape[1] // MXU_COLS):
        csl = slice(c * MXU_COLS, (c + 1) * MXU_COLS)
        o_ref[:, csl] = jax.nn.sigmoid(jnp.dot(h, w_ref[:, csl], preferred_element_type=F32)).astype(BF16)


def _gate_proj(h, w_gate):
    T, D = h.shape
    N = w_gate.shape[1]
    tm = PROJ_TM
    return pl.pallas_call(
        _gate_kernel,
        grid=(T // tm,),
        in_specs=[pl.BlockSpec((tm, D), lambda i: (i, 0)), _resident((D, N))],
        out_specs=pl.BlockSpec((tm, N), lambda i: (i, 0)),
        out_shape=jax.ShapeDtypeStruct((T, N), BF16),
        compiler_params=pltpu.CompilerParams(
            dimension_semantics=("arbitrary",), vmem_limit_bytes=VMEM_LIMIT_BYTES),
        name="gate_proj",
    )(h, w_gate)


MOBA_PAIR = 2
MOBA_UNROLL = 4
MASK_BIG = 4 * MOBA_BLOCK


def _moba_items(nb):
    qbs, prs = [], []
    for qb in range(nb):
        for pr in range(qb // MOBA_PAIR, -1, -1):
            qbs.append(qb)
            prs.append(pr)
    assert len(qbs) % MOBA_UNROLL == 0
    pad = 2 * MOBA_UNROLL
    tqb = np.asarray([nb] * pad + qbs + [nb] * pad, np.int32)
    tpr = np.asarray([0] * pad + prs + [0] * pad, np.int32)
    return tqb, tpr, len(qbs) // MOBA_UNROLL


def _moba_kernel(nb, n_trips, tqb_ref, tpr_ref, q_ref, k_ref, vt_ref, km_ref, o_ref,
                 thr_scr, s_scr, p_scr, acc_scr, l_scr):
    blk = MOBA_BLOCK
    span = MOBA_PAIR * blk

    km = km_ref[...]
    km_hi = km.astype(BF16)
    km_lo = (km - km_hi.astype(F32)).astype(BF16)
    bidx = lax.broadcasted_iota(jnp.int32, (nb, blk), 0)
    for qb in range(nb):
        q = q_ref[qb * blk:(qb + 1) * blk, :]
        gate = (lax.dot_general(km_hi, q, NT_DIMS, preferred_element_type=F32)
                + lax.dot_general(km_lo, q, NT_DIMS, preferred_element_type=F32))
        g = jnp.where(bidx < qb, gate, NEG_INF)
        thr = jnp.full((nb, blk), -MASK_BIG, jnp.int32)
        for _ in range(MOBA_TOPK):
            top = jnp.max(g, axis=0, keepdims=True)
            first = jnp.min(jnp.where(g == top, bidx, nb), axis=0, keepdims=True)
            hit = bidx == first
            thr = jnp.where(hit & (top > NEG_INF), MASK_BIG, thr)
            g = jnp.where(hit, NEG_INF, g)
        thr_scr[qb] = jnp.where(bidx == qb, 0, thr)
    thr_scr[nb] = jnp.full((nb, blk), MASK_BIG, jnp.int32)

    acc_scr[...] = jnp.zeros(acc_scr.shape, F32)
    s_scr[...] = jnp.zeros(s_scr.shape, F32)
    p_scr[...] = jnp.zeros(p_scr.shape, BF16)
    offs = (lax.broadcasted_iota(jnp.int32, (blk, blk), 0)
            - lax.broadcasted_iota(jnp.int32, (blk, blk), 1))

    def scores(idx, slot):
        qb, pr = tqb_ref[idx], tpr_ref[idx]
        qrow = pl.multiple_of(jnp.minimum(qb, nb - 1) * blk, blk)
        krow = pl.multiple_of(pr * span, span)
        s_scr[slot] = lax.dot_general(k_ref[pl.ds(krow, span), :], q_ref[pl.ds(qrow, blk), :], NT_DIMS,
                                      preferred_element_type=F32)

    def softmax(idx, slot, m, l):
        qb, pr = tqb_ref[idx], tpr_ref[idx]
        first = pr == qb // MOBA_PAIR
        m = jnp.where(first, NEG_INF, m)
        l = jnp.where(first, 0.0, l)
        s = s_scr[slot]
        s = jnp.concatenate(
            [jnp.where(offs <= thr_scr[qb, pl.ds(pr * MOBA_PAIR + u, 1), :], s[u * blk:(u + 1) * blk], NEG_INF)
             for u in range(MOBA_PAIR)], axis=0)
        m_new = jnp.maximum(m, jnp.max(s, axis=0, keepdims=True))
        alpha = jnp.exp2(m - m_new)
        p = jnp.exp2(s - m_new)
        l = alpha * l + jnp.sum(p, axis=0, keepdims=True)
        p_scr[slot] = p.astype(BF16)
        l_scr[qb] = l
        return m_new, l, alpha

    def pv(idx, slot, alpha):
        qb, pr = tqb_ref[idx], tpr_ref[idx]
        upd = jnp.dot(vt_ref[pr], p_scr[slot], preferred_element_type=F32)
        acc_scr[qb] = alpha * acc_scr[qb] + upd

    def trip(t, carry):
        m, l, alphas = carry
        base = t * MOBA_UNROLL
        new_alphas = []
        for i in range(MOBA_UNROLL):
            pv(base + i, i, alphas[i])
            m, l, alpha = softmax(base + MOBA_UNROLL + i, i, m, l)
            new_alphas.append(alpha)
            scores(base + 2 * MOBA_UNROLL + i, i)
        return m, l, tuple(new_alphas)

    row0 = jnp.zeros((1, blk), F32)
    lax.fori_loop(0, n_trips + 2, trip, (row0, row0, (row0,) * MOBA_UNROLL))

    for qb in range(nb):
        o_ref[qb * blk:(qb + 1) * blk, :] = (acc_scr[qb] * (1.0 / l_scr[qb])).T.astype(BF16)


def _moba_attention(q, k, vt, kmean, batch, seq):
    T = q.shape[0]
    blk = MOBA_BLOCK
    nb = seq // blk
    span = MOBA_PAIR * blk
    assert nb % MOBA_PAIR == 0 and vt.shape == (T // span, ATTN_WIDTH, span)
    tqb, tpr, n_trips = _moba_items(nb)
    head = lambda b, h, *_: (b, h)
    grid_spec = pltpu.PrefetchScalarGridSpec(
        num_scalar_prefetch=2,
        grid=(batch, N_ATTN_HEADS),
        in_specs=[
            pl.BlockSpec((seq, HEAD_DIM), head),
            pl.BlockSpec((seq, HEAD_DIM), head),
            pl.BlockSpec((seq // span, HEAD_DIM, span), lambda b, h, *_: (b, h, 0)),
            pl.BlockSpec((nb, HEAD_DIM), head),
        ],
        out_specs=pl.BlockSpec((seq, HEAD_DIM), head),
        scratch_shapes=[
            pltpu.VMEM((nb + 1, nb, blk), jnp.int32),
            pltpu.VMEM((MOBA_UNROLL, span, blk), F32),
            pltpu.VMEM((MOBA_UNROLL, span, blk), BF16),
            pltpu.VMEM((nb + 1, HEAD_DIM, blk), F32),
            pltpu.VMEM((nb + 1, 1, blk), F32),
        ],
    )
    return pl.pallas_call(
        functools.partial(_moba_kernel, nb, n_trips),
        grid_spec=grid_spec,
        out_shape=jax.ShapeDtypeStruct((T, ATTN_WIDTH), BF16),
        compiler_params=pltpu.CompilerParams(
            dimension_semantics=("arbitrary", "arbitrary"), vmem_limit_bytes=VMEM_LIMIT_BYTES),
        name="moba_attention",
    )(jnp.asarray(tqb), jnp.asarray(tpr), q, k, vt, kmean)


MIX_TM = 256


def _mix_kernel(a_ref, z_ref, sga_ref, sgc_ref, x_ref, wa_ref, wc_ref, wm_ref, o_ref):
    y_attn = jnp.dot(a_ref[...], wa_ref[...], preferred_element_type=F32)
    y_conv = jnp.dot(z_ref[...], wc_ref[...], preferred_element_type=F32)
    merged = sga_ref[...].astype(F32) * y_attn + sgc_ref[...].astype(F32) * y_conv
    o_ref[...] = x_ref[...] + jnp.dot(merged.astype(BF16), wm_ref[...], preferred_element_type=F32)


def _mix(attn, z, gates, xf, w_attn_out, w_conv_out, w_mix_out):
    T, D = xf.shape
    tm = MIX_TM
    row = lambda i: (i, 0)
    return pl.pallas_call(
        _mix_kernel,
        grid=(T // tm,),
        in_specs=[
            pl.BlockSpec((tm, ATTN_WIDTH), row),
            pl.BlockSpec((tm, CONV_WIDTH), row),
            pl.BlockSpec((tm, D), row),
            pl.BlockSpec((tm, D), lambda i: (i, 1)),
            pl.BlockSpec((tm, D), row),
            _resident((ATTN_WIDTH, D)),
            _resident((CONV_WIDTH, D)),
            _resident((D, D)),
        ],
        out_specs=pl.BlockSpec((tm, D), row),
        out_shape=jax.ShapeDtypeStruct((T, D), F32),
        compiler_params=pltpu.CompilerParams(
            dimension_semantics=("arbitrary",), vmem_limit_bytes=VMEM_LIMIT_BYTES),
        name="mix",
    )(attn, z, gates, gates, xf, w_attn_out, w_conv_out, w_mix_out)


def _mem_kv_kernel(mem_ref, g_ref, w_ref, o_ref):
    o_ref[...] = jnp.dot(_rms_norm_bf16(mem_ref[...], g_ref[...]), w_ref[...],
                         preferred_element_type=F32).astype(BF16)


def _mem_kv(memf, g, wkv):
    R, D = memf.shape
    N = wkv.shape[1]
    return pl.pallas_call(
        _mem_kv_kernel,
        grid=(1,),
        in_specs=[pl.BlockSpec((R, D), lambda i: (0, 0)),
                  pl.BlockSpec((1, D), lambda i: (0, 0)),
                  pl.BlockSpec((D, N), lambda i: (0, 0))],
        out_specs=pl.BlockSpec((R, N), lambda i: (0, 0)),
        out_shape=jax.ShapeDtypeStruct((R, N), BF16),
        compiler_params=pltpu.CompilerParams(vmem_limit_bytes=VMEM_LIMIT_BYTES),
        name="mem_kv",
    )(memf, g, wkv)


XATTN_TM = 512


def _xattn_kernel(x_ref, g_ref, wq_ref, kv_ref, wo_ref, o_ref):
    xf = x_ref[...]
    hq = jnp.dot(_rms_norm_bf16(xf, g_ref[...]), wq_ref[...], preferred_element_type=F32)
    hq = (hq * (ATTN_SCALE * LOG2E)).astype(BF16)
    outs = []
    for hh in range(N_XATTN_HEADS):
        sl = slice(hh * HEAD_DIM, (hh + 1) * HEAD_DIM)
        mk = kv_ref[:, sl]
        mv = kv_ref[:, XATTN_WIDTH + hh * HEAD_DIM:XATTN_WIDTH + (hh + 1) * HEAD_DIM]
        s = lax.dot_general(hq[:, sl], mk, NT_DIMS, preferred_element_type=F32)
        p = jnp.exp2(s - jnp.max(s, axis=-1, keepdims=True))
        inv = 1.0 / jnp.sum(p, axis=-1, keepdims=True)
        outs.append((jnp.dot(p.astype(BF16), mv, preferred_element_type=F32) * inv).astype(BF16))
    att = jnp.concatenate(outs, axis=-1)
    o_ref[...] = xf + jnp.dot(att, wo_ref[...], preferred_element_type=F32)


def _xattn(x1, g, wq, kv, wo, seq, n_mem):
    T, D = x1.shape
    tm = XATTN_TM
    tpb = seq // tm
    return pl.pallas_call(
        _xattn_kernel,
        grid=(T // tm,),
        in_specs=[
            pl.BlockSpec((tm, D), lambda i: (i, 0)),
            pl.BlockSpec((1, D), lambda i: (0, 0)),
            _resident((D, XATTN_WIDTH)),
            pl.BlockSpec((n_mem, 2 * XATTN_WIDTH), lambda i: (i // tpb, 0)),
            _resident((XATTN_WIDTH, D)),
        ],
        out_specs=pl.BlockSpec((tm, D), lambda i: (i, 0)),
        out_shape=jax.ShapeDtypeStruct((T, D), F32),
        compiler_params=pltpu.CompilerParams(
            dimension_semantics=("arbitrary",), vmem_limit_bytes=VMEM_LIMIT_BYTES),
        name="xattn",
    )(x1, g, wq, kv, wo)


MLP_TM = 512
MLP_TF = 1024


def _mlp_kernel(x_ref, g_ref, wu_ref, wd_ref, gf_ref, o_ref, h_scr, acc_scr):
    f = pl.program_id(1)

    @pl.when(f == 0)
    def _():
        xf = x_ref[...]
        h_scr[...] = _rms_norm_bf16(xf, g_ref[...])
        acc_scr[...] = xf

    u = jnp.maximum(jnp.dot(h_scr[...], wu_ref[...], preferred_element_type=F32), 0.0)
    acc_scr[...] += jnp.dot((u * u).astype(BF16), wd_ref[...], preferred_element_type=F32)

    @pl.when(f == pl.num_programs(1) - 1)
    def _():
        y = acc_scr[...]
        ms = jnp.mean(y * y, axis=-1, keepdims=True)
        o_ref[...] = y * lax.rsqrt(ms + NORM_EPS) * gf_ref[...]


def _mlp(x2, g, w_up, w_down, g_final):
    T, D = x2.shape
    FF = w_up.shape[1]
    tm, tf = MLP_TM, MLP_TF
    return pl.pallas_call(
        _mlp_kernel,
        grid=(T // tm, FF // tf),
        in_specs=[
            pl.BlockSpec((tm, D), lambda i, f: (i, 0)),
            pl.BlockSpec((1, D), lambda i, f: (0, 0)),
            pl.BlockSpec((D, tf), lambda i, f: (0, f)),
            pl.BlockSpec((tf, D), lambda i, f: (f, 0)),
            pl.BlockSpec((1, D), lambda i, f: (0, 0)),
        ],
        out_specs=pl.BlockSpec((tm, D), lambda i, f: (i, 0)),
        out_shape=jax.ShapeDtypeStruct((T, D), F32),
        scratch_shapes=[pltpu.VMEM((tm, D), BF16), pltpu.VMEM((tm, D), F32)],
        compiler_params=pltpu.CompilerParams(
            dimension_semantics=("arbitrary", "arbitrary"), vmem_limit_bytes=VMEM_LIMIT_BYTES),
        name="mlp",
    )(x2, g, w_up, w_down, g_final)


def _rope_tables(seq):
    half = ROPE_DIM // 2
    inv_freq = ROPE_THETA ** (-jnp.arange(half, dtype=F32) / half)
    ang = jnp.arange(seq).astype(F32)[:, None] * inv_freq[None, :]
    cos, sin = jnp.cos(ang), jnp.sin(ang)
    pad = HEAD_DIM - ROPE_DIM
    cos_t = jnp.concatenate([cos, cos, jnp.ones((seq, pad), F32)], axis=-1)
    sin_lo = jnp.concatenate([-sin, jnp.zeros((seq, half + pad), F32)], axis=-1)
    sin_hi = jnp.concatenate([jnp.zeros((seq, half), F32), sin, jnp.zeros((seq, pad), F32)], axis=-1)
    return cos_t, sin_lo, sin_hi


def _layer(xf, memf, batch, seq, n_mem, norm_mix, w_in, conv_w, w_attn_out, w_conv_out, w_mix_out,
           norm_xattn, norm_mem, wq_x, wkv_x, wo_x, norm_mlp, w_up, w_down, g_out):
    T, D = xf.shape
    bf = lambda w: w.astype(BF16)
    row = lambda g: g.reshape(1, D).astype(F32)
    cos_t, sin_lo, sin_hi = _rope_tables(seq)
    n_qkv, n_conv = 3 * ATTN_WIDTH, 3 * CONV_WIDTH
    h = _norm(xf, row(norm_mix))
    q, k, vt, kmean = _qkv_proj(h, bf(w_in[:, :n_qkv]), cos_t, sin_lo, sin_hi, seq)
    z = _conv_proj(h, bf(w_in[:, n_qkv:n_qkv + n_conv]), conv_w, seq)
    gates = _gate_proj(h, bf(w_in[:, n_qkv + n_conv:]))
    kmean = kmean.reshape(T // MOBA_BLOCK, ATTN_WIDTH)
    attn = _moba_attention(q, k, vt, kmean, batch, seq)
    x1 = _mix(attn, z, gates, xf, bf(w_attn_out), bf(w_conv_out), bf(w_mix_out))
    kv = _mem_kv(memf, row(norm_mem), bf(wkv_x))
    x2 = _xattn(x1, row(norm_xattn), bf(wq_x), kv, bf(wo_x), seq, n_mem)
    return _mlp(x2, row(norm_mlp), bf(w_up), bf(w_down), g_out)


def kernel(x, mem, norm_mix, w_in, conv_w, w_attn_out, w_conv_out, w_mix_out, norm_xattn, norm_mem,
           wq_x, wkv_x, wo_x, norm_mlp, w_up, w_down, norm_final):
    B, S, D = x.shape
    n_mem = mem.shape[1]
    depth = w_in.shape[0]
    assert depth == 1, "the final rmsnorm is fused into the last layer's MLP kernel"
    xf = x.reshape(B * S, D)
    memf = mem.reshape(B * n_mem, D)
    out = _layer(xf, memf, B, S, n_mem, norm_mix[0], w_in[0], conv_w[0], w_attn_out[0], w_conv_out[0],
                 w_mix_out[0], norm_xattn[0], norm_mem[0], wq_x[0], wkv_x[0], wo_x[0], norm_mlp[0],
                 w_up[0], w_down[0], norm_final.reshape(1, D).astype(F32))
    return out.reshape(B, S, D)
```

```python
import functools
import math

import jax
import jax.numpy as jnp
import numpy as np
from jax import lax
from jax.experimental import pallas as pl
from jax.experimental.pallas import tpu as pltpu

F32 = jnp.float32
BF16 = jnp.bfloat16

HEAD_DIM = 128
N_ATTN_HEADS = 8
ATTN_WIDTH = N_ATTN_HEADS * HEAD_DIM
CONV_WIDTH = 1024
CONV_K = 3
MOBA_BLOCK = 256
MOBA_TOPK = 3
ROPE_THETA = 500000.0
ROPE_DIM = HEAD_DIM // 4
N_XATTN_HEADS = 4
XATTN_WIDTH = N_XATTN_HEADS * HEAD_DIM
NORM_EPS = 1e-6

LOG2E = math.log2(math.e)
ATTN_SCALE = HEAD_DIM ** -0.5
NEG_INF = float("-inf")

VMEM_LIMIT_BYTES = 52 * 1024 * 1024
SUBLANES = 8
MXU_COLS = 256

NT_DIMS = (((1,), (1,)), ((), ()))
TN_DIMS = (((0,), (0,)), ((), ()))


def _rms_norm_bf16(xf, g):
    ms = jnp.mean(xf * xf, axis=-1, keepdims=True)
    return (xf * lax.rsqrt(ms + NORM_EPS) * g).astype(BF16)


def _resident(shape):
    return pl.BlockSpec(shape, lambda *_: (0,) * len(shape), pipeline_mode=pl.Buffered(1))


NORM_TM = 1024


def _norm_kernel(x_ref, g_ref, o_ref):
    o_ref[...] = _rms_norm_bf16(x_ref[...], g_ref[...])


def _norm(xf, g):
    T, D = xf.shape
    tm = NORM_TM
    return pl.pallas_call(
        _norm_kernel,
        grid=(T // tm,),
        in_specs=[pl.BlockSpec((tm, D), lambda i: (i, 0)), pl.BlockSpec((1, D), lambda i: (0, 0))],
        out_specs=pl.BlockSpec((tm, D), lambda i: (i, 0)),
        out_shape=jax.ShapeDtypeStruct((T, D), BF16),
        compiler_params=pltpu.CompilerParams(
            dimension_semantics=("arbitrary",), vmem_limit_bytes=VMEM_LIMIT_BYTES),
        name="norm_mix",
    )(xf, g)


PROJ_TM = 512


def _rope(t, cos, sin_lo, sin_hi):
    return (t * cos + pltpu.roll(t, HEAD_DIM - ROPE_DIM // 2, 1) * sin_lo
            + pltpu.roll(t, ROPE_DIM // 2, 1) * sin_hi)


def _qkv_kernel(h_ref, w_ref, cos_ref, slo_ref, shi_ref, q_ref, k_ref, vt_ref, km_ref):
    tm = PROJ_TM
    h = h_ref[...]
    cos, slo, shi = cos_ref[...], slo_ref[...], shi_ref[...]
    heads_per_chunk = MXU_COLS // HEAD_DIM
    for c in range(ATTN_WIDTH // MXU_COLS):
        csl = slice(c * MXU_COLS, (c + 1) * MXU_COLS)
        qc = jnp.dot(h, w_ref[:, csl], preferred_element_type=F32)
        kc = jnp.dot(h, w_ref[:, ATTN_WIDTH + c * MXU_COLS:ATTN_WIDTH + (c + 1) * MXU_COLS],
                     preferred_element_type=F32)
        vc = jnp.dot(h, w_ref[:, 2 * ATTN_WIDTH + c * MXU_COLS:2 * ATTN_WIDTH + (c + 1) * MXU_COLS],
                     preferred_element_type=F32)
        vt_ref[0, csl, :] = vc.T.astype(BF16)
        for hh in range(heads_per_chunk):
            sl = slice(hh * HEAD_DIM, (hh + 1) * HEAD_DIM)
            osl = slice(c * MXU_COLS + hh * HEAD_DIM, c * MXU_COLS + (hh + 1) * HEAD_DIM)
            q_ref[:, osl] = (_rope(qc[:, sl], cos, slo, shi) * (ATTN_SCALE * LOG2E)).astype(BF16)
            kf = _rope(kc[:, sl], cos, slo, shi)
            k_ref[:, osl] = kf.astype(BF16)
            for r in range(tm // MOBA_BLOCK):
                km_ref[r, :, osl] = jnp.mean(kf[r * MOBA_BLOCK:(r + 1) * MOBA_BLOCK], axis=0, keepdims=True)


def _qkv_proj(h, w_qkv, cos, slo, shi, seq):
    T, D = h.shape
    tm = PROJ_TM
    tpb = seq // tm
    row = lambda i: (i, 0)
    pos = lambda i: (i % tpb, 0)
    return pl.pallas_call(
        _qkv_kernel,
        grid=(T // tm,),
        in_specs=[
            pl.BlockSpec((tm, D), row),
            _resident((D, 3 * ATTN_WIDTH)),
            pl.BlockSpec((tm, HEAD_DIM), pos),
            pl.BlockSpec((tm, HEAD_DIM), pos),
            pl.BlockSpec((tm, HEAD_DIM), pos),
        ],
        out_specs=(
            pl.BlockSpec((tm, ATTN_WIDTH), row),
            pl.BlockSpec((tm, ATTN_WIDTH), row),
            pl.BlockSpec((1, ATTN_WIDTH, tm), lambda i: (i, 0, 0)),
            pl.BlockSpec((tm // MOBA_BLOCK, 1, ATTN_WIDTH), lambda i: (i, 0, 0)),
        ),
        out_shape=(
            jax.ShapeDtypeStruct((T, ATTN_WIDTH), BF16),
            jax.ShapeDtypeStruct((T, ATTN_WIDTH), BF16),
            jax.ShapeDtypeStruct((T // tm, ATTN_WIDTH, tm), BF16),
            jax.ShapeDtypeStruct((T // MOBA_BLOCK, 1, ATTN_WIDTH), F32),
        ),
        compiler_params=pltpu.CompilerParams(
            dimension_semantics=("arbitrary",), vmem_limit_bytes=VMEM_LIMIT_BYTES),
        name="qkv_proj",
    )(h, w_qkv, cos, slo, shi)


def _conv_kernel(tiles_per_batch, h_ref, w_ref, cw_ref, z_ref, u_scr):
    tm = PROJ_TM
    i = pl.program_id(0)

    @pl.when(i % tiles_per_batch == 0)
    def _():
        u_scr[0:SUBLANES, :] = jnp.zeros((SUBLANES, CONV_WIDTH), F32)

    h = h_ref[...]
    for c in range(CONV_WIDTH // MXU_COLS):
        csl = slice(c * MXU_COLS, (c + 1) * MXU_COLS)
        cx = jnp.dot(h, w_ref[:, csl], preferred_element_type=F32)
        cb = jnp.dot(h, w_ref[:, CONV_WIDTH + c * MXU_COLS:CONV_WIDTH + (c + 1) * MXU_COLS],
                     preferred_element_type=F32)
        cc = jnp.dot(h, w_ref[:, 2 * CONV_WIDTH + c * MXU_COLS:2 * CONV_WIDTH + (c + 1) * MXU_COLS],
                     preferred_element_type=F32)
        u = cc * cx
        u_scr[SUBLANES:tm + SUBLANES, csl] = u
        conv = (cw_ref[0:1, csl] * u_scr[SUBLANES - 2:tm + SUBLANES - 2, csl]
                + cw_ref[1:2, csl] * u_scr[SUBLANES - 1:tm + SUBLANES - 1, csl]
                + cw_ref[2:3, csl] * u)
        z_ref[:, csl] = (cb * conv).astype(BF16)
        u_scr[0:SUBLANES, csl] = u[tm - SUBLANES:tm]


def _conv_proj(h, w_conv, conv_w, seq):
    T, D = h.shape
    tm = PROJ_TM
    return pl.pallas_call(
        functools.partial(_conv_kernel, seq // tm),
        grid=(T // tm,),
        in_specs=[
            pl.BlockSpec((tm, D), lambda i: (i, 0)),
            _resident((D, 3 * CONV_WIDTH)),
            pl.BlockSpec((CONV_K, CONV_WIDTH), lambda i: (0, 0)),
        ],
        out_specs=pl.BlockSpec((tm, CONV_WIDTH), lambda i: (i, 0)),
        out_shape=jax.ShapeDtypeStruct((T, CONV_WIDTH), BF16),
        scratch_shapes=[pltpu.VMEM((tm + SUBLANES, CONV_WIDTH), F32)],
        compiler_params=pltpu.CompilerParams(
            dimension_semantics=("arbitrary",), vmem_limit_bytes=VMEM_LIMIT_BYTES),
        name="conv_proj",
    )(h, w_conv, conv_w)


def _gate_kernel(h_ref, w_ref, o_ref):
    h = h_ref[...]
    for c in range(o_ref.shape[1] // MXU_COLS):
        csl = slice(c * MXU_COLS, (c + 1) * MXU_COLS)
        o_ref[:, csl] = jax.nn.sigmoid(jnp.dot(h, w_ref[:, csl], preferred_element_type=F32)).astype(BF16)


def _gate_proj(h, w_gate):
    T, D = h.shape
    N = w_gate.shape[1]
    tm = PROJ_TM
    return pl.pallas_call(
        _gate_kernel,
        grid=(T // tm,),
        in_specs=[pl.BlockSpec((tm, D), lambda i: (i, 0)), _resident((D, N))],
        out_specs=pl.BlockSpec((tm, N), lambda i: (i, 0)),
        out_shape=jax.ShapeDtypeStruct((T, N), BF16),
        compiler_params=pltpu.CompilerParams(
            dimension_semantics=("arbitrary",), vmem_limit_bytes=VMEM_LIMIT_BYTES),
        name="gate_proj",
    )(h, w_gate)


MOBA_PAIR = 2
MOBA_UNROLL = 4
MASK_NEG = -1e30
ONES_ROWS = 16


def _moba_items(nb):
    qbs, prs = [], []
    for qb in range(nb):
        for pr in range((qb + MOBA_PAIR - 1) // MOBA_PAIR):
            qbs.append(qb)
            prs.append(pr)
    assert len(qbs) % MOBA_UNROLL == 0
    pad = 2 * MOBA_UNROLL
    tqb = np.asarray([nb] * pad + qbs + [nb] * pad, np.int32)
    tpr = np.asarray([0] * pad + prs + [0] * pad, np.int32)
    return tqb, tpr, len(qbs) // MOBA_UNROLL


def _moba_kernel(nb, n_trips, tqb_ref, tpr_ref, q_ref, k_ref, vt_ref, km_ref, hot_ref, o_ref,
                 qbias_scr, s_scr, p_scr, acc_scr, m_scr):
    blk = MOBA_BLOCK
    span = MOBA_PAIR * blk
    ones = jnp.ones((ONES_ROWS, span), BF16)

    km = km_ref[...]
    km_hi = km.astype(BF16)
    km_lo = (km - km_hi.astype(F32)).astype(BF16)
    bidx = lax.broadcasted_iota(jnp.int32, (nb, blk), 0)
    causal = (lax.broadcasted_iota(jnp.int32, (blk, blk), 0)
              <= lax.broadcasted_iota(jnp.int32, (blk, blk), 1))
    def choose_and_score(qb):
        rows = slice(qb * blk, (qb + 1) * blk)
        q = q_ref[rows, :]
        gate = (lax.dot_general(km_hi, q, NT_DIMS, preferred_element_type=F32)
                + lax.dot_general(km_lo, q, NT_DIMS, preferred_element_type=F32))
        g = jnp.where(bidx < qb, gate, NEG_INF)
        bias = jnp.full((nb, blk), MASK_NEG, F32)
        for _ in range(MOBA_TOPK):
            top = jnp.max(g, axis=0, keepdims=True)
            first = jnp.min(jnp.where(g == top, bidx, nb), axis=0, keepdims=True)
            hit = bidx == first
            bias = jnp.where(hit & (top > NEG_INF), 0.0, bias)
            g = jnp.where(hit, NEG_INF, g)
        bias = jnp.concatenate([bias, jnp.zeros((HEAD_DIM - nb, blk), F32)], axis=0)
        qbias_scr[rows, :] = bias.T.astype(BF16)
        return lax.dot_general(k_ref[rows, :], q, NT_DIMS, preferred_element_type=F32)

    def own_block(qb, s):
        s = jnp.where(causal, s, NEG_INF)
        m = jnp.max(s, axis=0, keepdims=True)
        p = jnp.exp2(s - m).astype(BF16)
        col = (qb % MOBA_PAIR) * blk
        vt = jnp.concatenate([vt_ref[qb // MOBA_PAIR, :, col:col + blk], ones[:, :blk]], axis=0)
        acc_scr[qb] = jnp.dot(vt, p, preferred_element_type=F32)
        m_scr[qb] = m

    own_lag = 2
    scored = [choose_and_score(qb) for qb in range(min(own_lag, nb))]
    for qb in range(nb):
        if qb + own_lag < nb:
            scored.append(choose_and_score(qb + own_lag))
        own_block(qb, scored[qb])

    acc_scr[nb] = jnp.zeros(acc_scr.shape[1:], F32)
    m_scr[nb] = jnp.zeros((1, blk), F32)
    s_scr[...] = jnp.zeros(s_scr.shape, F32)
    p_scr[...] = jnp.zeros(p_scr.shape, BF16)

    def scores(idx, slot):
        qb, pr = tqb_ref[idx], tpr_ref[idx]
        qrow = pl.multiple_of(jnp.minimum(qb, nb - 1) * blk, blk)
        krow = pl.multiple_of(pr * span, span)
        k_ext = jnp.concatenate([k_ref[pl.ds(krow, span), :], hot_ref[pl.ds(krow, span), :]], axis=1)
        q_ext = jnp.concatenate([q_ref[pl.ds(qrow, blk), :], qbias_scr[pl.ds(qrow, blk), :]], axis=1)
        s_scr[slot] = lax.dot_general(k_ext, q_ext, NT_DIMS, preferred_element_type=F32)

    def softmax(idx, slot, m):
        qb, pr = tqb_ref[idx], tpr_ref[idx]
        m = jnp.where(pr == 0, m_scr[qb], m)
        s = s_scr[slot]
        m_new = jnp.maximum(m, jnp.max(s, axis=0, keepdims=True))
        p_scr[slot] = jnp.exp2(s - m_new).astype(BF16)
        return m_new, jnp.exp2(m - m_new)

    def pv(idx, slot, alpha):
        qb, pr = tqb_ref[idx], tpr_ref[idx]
        vt = jnp.concatenate([vt_ref[pr], ones], axis=0)
        acc_scr[qb] = alpha * acc_scr[qb] + jnp.dot(vt, p_scr[slot], preferred_element_type=F32)

    def trip(t, carry):
        m, alphas = carry
        base = t * MOBA_UNROLL
        new_alphas = []
        for i in range(MOBA_UNROLL):
            pv(base + i, i, alphas[i])
            m, alpha = softmax(base + MOBA_UNROLL + i, i, m)
            new_alphas.append(alpha)
            scores(base + 2 * MOBA_UNROLL + i, i)
        return m, tuple(new_alphas)

    row0 = jnp.zeros((1, blk), F32)
    lax.fori_loop(0, n_trips + 2, trip, (row0, (row0,) * MOBA_UNROLL))

    for qb in range(nb):
        acc = acc_scr[qb]
        inv = 1.0 / acc[HEAD_DIM:HEAD_DIM + 1]
        o_ref[qb * blk:(qb + 1) * blk, :] = (acc[:HEAD_DIM] * inv).T.astype(BF16)


def _moba_attention(q, k, vt, kmean, batch, seq):
    T = q.shape[0]
    blk = MOBA_BLOCK
    nb = seq // blk
    span = MOBA_PAIR * blk
    assert nb % MOBA_PAIR == 0 and nb <= HEAD_DIM and vt.shape == (T // span, ATTN_WIDTH, span)
    tqb, tpr, n_trips = _moba_items(nb)
    hot = jnp.asarray(np.arange(seq)[:, None] // blk == np.arange(HEAD_DIM)[None, :], BF16)
    head = lambda b, h, *_: (b, h)
    grid_spec = pltpu.PrefetchScalarGridSpec(
        num_scalar_prefetch=2,
        grid=(batch, N_ATTN_HEADS),
        in_specs=[
            pl.BlockSpec((seq, HEAD_DIM), head),
            pl.BlockSpec((seq, HEAD_DIM), head),
            pl.BlockSpec((seq // span, HEAD_DIM, span), lambda b, h, *_: (b, h, 0)),
            pl.BlockSpec((nb, HEAD_DIM), head),
            _resident((seq, HEAD_DIM)),
        ],
        out_specs=pl.BlockSpec((seq, HEAD_DIM), head),
        scratch_shapes=[
            pltpu.VMEM((seq, HEAD_DIM), BF16),
            pltpu.VMEM((MOBA_UNROLL, span, blk), F32),
            pltpu.VMEM((MOBA_UNROLL, span, blk), BF16),
            pltpu.VMEM((nb + 1, HEAD_DIM + ONES_ROWS, blk), F32),
            pltpu.VMEM((nb + 1, 1, blk), F32),
        ],
    )
    return pl.pallas_call(
        functools.partial(_moba_kernel, nb, n_trips),
        grid_spec=grid_spec,
        out_shape=jax.ShapeDtypeStruct((T, ATTN_WIDTH), BF16),
        compiler_params=pltpu.CompilerParams(
            dimension_semantics=("arbitrary", "arbitrary"), vmem_limit_bytes=VMEM_LIMIT_BYTES),
        name="moba_attention",
    )(jnp.asarray(tqb), jnp.asarray(tpr), q, k, vt, kmean, hot)


MIX_TM = 256


def _mix_kernel(a_ref, z_ref, sga_ref, sgc_ref, x_ref, wa_ref, wc_ref, wm_ref, o_ref):
    y_attn = jnp.dot(a_ref[...], wa_ref[...], preferred_element_type=F32)
    y_conv = jnp.dot(z_ref[...], wc_ref[...], preferred_element_type=F32)
    merged = sga_ref[...].astype(F32) * y_attn + sgc_ref[...].astype(F32) * y_conv
    o_ref[...] = x_ref[...] + jnp.dot(merged.astype(BF16), wm_ref[...], preferred_element_type=F32)


def _mix(attn, z, gates, xf, w_attn_out, w_conv_out, w_mix_out):
    T, D = xf.shape
    tm = MIX_TM
    row = lambda i: (i, 0)
    return pl.pallas_call(
        _mix_kernel,
        grid=(T // tm,),
        in_specs=[
            pl.BlockSpec((tm, ATTN_WIDTH), row),
            pl.BlockSpec((tm, CONV_WIDTH), row),
            pl.BlockSpec((tm, D), row),
            pl.BlockSpec((tm, D), lambda i: (i, 1)),
            pl.BlockSpec((tm, D), row),
            _resident((ATTN_WIDTH, D)),
            _resident((CONV_WIDTH, D)),
            _resident((D, D)),
        ],
        out_specs=pl.BlockSpec((tm, D), row),
        out_shape=jax.ShapeDtypeStruct((T, D), F32),
        compiler_params=pltpu.CompilerParams(
            dimension_semantics=("arbitrary",), vmem_limit_bytes=VMEM_LIMIT_BYTES),
        name="mix",
    )(attn, z, gates, gates, xf, w_attn_out, w_conv_out, w_mix_out)


def _mem_kv_kernel(mem_ref, g_ref, w_ref, o_ref):
    o_ref[...] = jnp.dot(_rms_norm_bf16(mem_ref[...], g_ref[...]), w_ref[...],
                         preferred_element_type=F32).astype(BF16)


def _mem_kv(memf, g, wkv):
    R, D = memf.shape
    N = wkv.shape[1]
    return pl.pallas_call(
        _mem_kv_kernel,
        grid=(1,),
        in_specs=[pl.BlockSpec((R, D), lambda i: (0, 0)),
                  pl.BlockSpec((1, D), lambda i: (0, 0)),
                  pl.BlockSpec((D, N), lambda i: (0, 0))],
        out_specs=pl.BlockSpec((R, N), lambda i: (0, 0)),
        out_shape=jax.ShapeDtypeStruct((R, N), BF16),
        compiler_params=pltpu.CompilerParams(vmem_limit_bytes=VMEM_LIMIT_BYTES),
        name="mem_kv",
    )(memf, g, wkv)


XATTN_TM = 512


def _xattn_kernel(x_ref, g_ref, wq_ref, kv_ref, wo_ref, o_ref):
    xf = x_ref[...]
    hq = jnp.dot(_rms_norm_bf16(xf, g_ref[...]), wq_ref[...], preferred_element_type=F32)
    hq = (hq * (ATTN_SCALE * LOG2E)).astype(BF16)
    outs = []
    for hh in range(N_XATTN_HEADS):
        sl = slice(hh * HEAD_DIM, (hh + 1) * HEAD_DIM)
        mk = kv_ref[:, sl]
        mv = kv_ref[:, XATTN_WIDTH + hh * HEAD_DIM:XATTN_WIDTH + (hh + 1) * HEAD_DIM]
        s = lax.dot_general(hq[:, sl], mk, NT_DIMS, preferred_element_type=F32)
        p = jnp.exp2(s - jnp.max(s, axis=-1, keepdims=True))
        inv = 1.0 / jnp.sum(p, axis=-1, keepdims=True)
        outs.append((jnp.dot(p.astype(BF16), mv, preferred_element_type=F32) * inv).astype(BF16))
    att = jnp.concatenate(outs, axis=-1)
    o_ref[...] = xf + jnp.dot(att, wo_ref[...], preferred_element_type=F32)


def _xattn(x1, g, wq, kv, wo, seq, n_mem):
    T, D = x1.shape
    tm = XATTN_TM
    tpb = seq // tm
    return pl.pallas_call(
        _xattn_kernel,
        grid=(T // tm,),
        in_specs=[
            pl.BlockSpec((tm, D), lambda i: (i, 0)),
            pl.BlockSpec((1, D), lambda i: (0, 0)),
            _resident((D, XATTN_WIDTH)),
            pl.BlockSpec((n_mem, 2 * XATTN_WIDTH), lambda i: (i // tpb, 0)),
            _resident((XATTN_WIDTH, D)),
        ],
        out_specs=pl.BlockSpec((tm, D), lambda i: (i, 0)),
        out_shape=jax.ShapeDtypeStruct((T, D), F32),
        compiler_params=pltpu.CompilerParams(
            dimension_semantics=("arbitrary",), vmem_limit_bytes=VMEM_LIMIT_BYTES),
        name="xattn",
    )(x1, g, wq, kv, wo)


MLP_TM = 512
MLP_TF = 1024


def _mlp_kernel(x_ref, g_ref, wu_ref, wd_ref, gf_ref, o_ref, h_scr, acc_scr):
    f = pl.program_id(1)

    @pl.when(f == 0)
    def _():
        xf = x_ref[...]
        h_scr[...] = _rms_norm_bf16(xf, g_ref[...])
        acc_scr[...] = xf

    u = jnp.maximum(jnp.dot(h_scr[...], wu_ref[...], preferred_element_type=F32), 0.0)
    acc_scr[...] += jnp.dot((u * u).astype(BF16), wd_ref[...], preferred_element_type=F32)

    @pl.when(f == pl.num_programs(1) - 1)
    def _():
        y = acc_scr[...]
        ms = jnp.mean(y * y, axis=-1, keepdims=True)
        o_ref[...] = y * lax.rsqrt(ms + NORM_EPS) * gf_ref[...]


def _mlp(x2, g, w_up, w_down, g_final):
    T, D = x2.shape
    FF = w_up.shape[1]
    tm, tf = MLP_TM, MLP_TF
    return pl.pallas_call(
        _mlp_kernel,
        grid=(T // tm, FF // tf),
        in_specs=[
            pl.BlockSpec((tm, D), lambda i, f: (i, 0)),
            pl.BlockSpec((1, D), lambda i, f: (0, 0)),
            pl.BlockSpec((D, tf), lambda i, f: (0, f)),
            pl.BlockSpec((tf, D), lambda i, f: (f, 0)),
            pl.BlockSpec((1, D), lambda i, f: (0, 0)),
        ],
        out_specs=pl.BlockSpec((tm, D), lambda i, f: (i, 0)),
        out_shape=jax.ShapeDtypeStruct((T, D), F32),
        scratch_shapes=[pltpu.VMEM((tm, D), BF16), pltpu.VMEM((tm, D), F32)],
        compiler_params=pltpu.CompilerParams(
            dimension_semantics=("arbitrary", "arbitrary"), vmem_limit_bytes=VMEM_LIMIT_BYTES),
        name="mlp",
    )(x2, g, w_up, w_down, g_final)


def _rope_tables(seq):
    half = ROPE_DIM // 2
    inv_freq = ROPE_THETA ** (-jnp.arange(half, dtype=F32) / half)
    ang = jnp.arange(seq).astype(F32)[:, None] * inv_freq[None, :]
    cos, sin = jnp.cos(ang), jnp.sin(ang)
    pad = HEAD_DIM - ROPE_DIM
    cos_t = jnp.concatenate([cos, cos, jnp.ones((seq, pad), F32)], axis=-1)
    sin_lo = jnp.concatenate([-sin, jnp.zeros((seq, half + pad), F32)], axis=-1)
    sin_hi = jnp.concatenate([jnp.zeros((seq, half), F32), sin, jnp.zeros((seq, pad), F32)], axis=-1)
    return cos_t, sin_lo, sin_hi


def _layer(xf, memf, batch, seq, n_mem, norm_mix, w_in, conv_w, w_attn_out, w_conv_out, w_mix_out,
           norm_xattn, norm_mem, wq_x, wkv_x, wo_x, norm_mlp, w_up, w_down, g_out):
    T, D = xf.shape
    bf = lambda w: w.astype(BF16)
    row = lambda g: g.reshape(1, D).astype(F32)
    cos_t, sin_lo, sin_hi = _rope_tables(seq)
    n_qkv, n_conv = 3 * ATTN_WIDTH, 3 * CONV_WIDTH
    h = _norm(xf, row(norm_mix))
    q, k, vt, kmean = _qkv_proj(h, bf(w_in[:, :n_qkv]), cos_t, sin_lo, sin_hi, seq)
    z = _conv_proj(h, bf(w_in[:, n_qkv:n_qkv + n_conv]), conv_w, seq)
    gates = _gate_proj(h, bf(w_in[:, n_qkv + n_conv:]))
    kmean = kmean.reshape(T // MOBA_BLOCK, ATTN_WIDTH)
    attn = _moba_attention(q, k, vt, kmean, batch, seq)
    x1 = _mix(attn, z, gates, xf, bf(w_attn_out), bf(w_conv_out), bf(w_mix_out))
    kv = _mem_kv(memf, row(norm_mem), bf(wkv_x))
    x2 = _xattn(x1, row(norm_xattn), bf(wq_x), kv, bf(wo_x), seq, n_mem)
    return _mlp(x2, row(norm_mlp), bf(w_up), bf(w_down), g_out)


def kernel(x, mem, norm_mix, w_in, conv_w, w_attn_out, w_conv_out, w_mix_out, norm_xattn, norm_mem,
           wq_x, wkv_x, wo_x, norm_mlp, w_up, w_down, norm_final):
    B, S, D = x.shape
    n_mem = mem.shape[1]
    depth = w_in.shape[0]
    assert depth == 1, "the final rmsnorm is fused into the last layer's MLP kernel"
    xf = x.reshape(B * S, D)
    memf = mem.reshape(B * n_mem, D)
    out = _layer(xf, memf, B, S, n_mem, norm_mix[0], w_in[0], conv_w[0], w_attn_out[0], w_conv_out[0],
                 w_mix_out[0], norm_xattn[0], norm_mem[0], wq_x[0], wkv_x[0], wo_x[0], norm_mlp[0],
                 w_up[0], w_down[0], norm_final.reshape(1, D).astype(F32))
    return out.reshape(B, S, D)
```

```python
import functools
import math

import jax
import jax.numpy as jnp
import numpy as np
from jax import lax
from jax.experimental import pallas as pl
from jax.experimental.pallas import tpu as pltpu

F32 = jnp.float32
BF16 = jnp.bfloat16

HEAD_DIM = 128
N_ATTN_HEADS = 8
ATTN_WIDTH = N_ATTN_HEADS * HEAD_DIM
CONV_WIDTH = 1024
CONV_K = 3
MOBA_BLOCK = 256
MOBA_TOPK = 3
ROPE_THETA = 500000.0
ROPE_DIM = HEAD_DIM // 4
N_XATTN_HEADS = 4
XATTN_WIDTH = N_XATTN_HEADS * HEAD_DIM
NORM_EPS = 1e-6

LOG2E = math.log2(math.e)
ATTN_SCALE = HEAD_DIM ** -0.5
NEG_INF = float("-inf")

VMEM_LIMIT_BYTES = 52 * 1024 * 1024
SUBLANES = 8
BF16_SUBLANES = 16
MXU_COLS = 256

NT_DIMS = (((1,), (1,)), ((), ()))


def _rms_norm_bf16(xf, g):
    ms = jnp.mean(xf * xf, axis=-1, keepdims=True)
    return (xf * lax.rsqrt(ms + NORM_EPS) * g).astype(BF16)


def _resident(shape):
    return pl.BlockSpec(shape, lambda *_: (0,) * len(shape), pipeline_mode=pl.Buffered(1))


def _cast_job(w, steps, n_cols=None, col_block=0, step_of=lambda i, *_: i):
    n_rows = w.shape[0]
    n_cols = w.shape[1] if n_cols is None else n_cols
    rows = n_rows // steps
    assert rows * steps == n_rows and rows % BF16_SUBLANES == 0
    in_spec = pl.BlockSpec((rows, n_cols), lambda *g: (step_of(*g), col_block))
    out_spec = pl.BlockSpec((rows, n_cols), lambda *g: (step_of(*g), 0))
    return w, in_spec, out_spec, jax.ShapeDtypeStruct((n_rows, n_cols), BF16)


def _call(body, name, grid, operands, in_specs, out_specs, out_shape, scratch_shapes=(), casts=(),
          prefetch=()):
    n_in = len(prefetch) + len(operands)
    n_out = len(out_shape)
    n_cast = len(casts)

    def kern(*refs):
        ins, refs = refs[:n_in], refs[n_in:]
        cast_src, refs = refs[:n_cast], refs[n_cast:]
        outs, refs = refs[:n_out], refs[n_out:]
        cast_dst, scratch = refs[:n_cast], refs[n_cast:]
        body(*ins, *outs, *scratch)
        for src, dst in zip(cast_src, cast_dst):
            dst[...] = src[...].astype(BF16)

    grid_spec = pltpu.PrefetchScalarGridSpec(
        num_scalar_prefetch=len(prefetch),
        grid=grid,
        in_specs=list(in_specs) + [c[1] for c in casts],
        out_specs=tuple(out_specs) + tuple(c[2] for c in casts),
        scratch_shapes=list(scratch_shapes),
    )
    outs = pl.pallas_call(
        kern,
        grid_spec=grid_spec,
        out_shape=tuple(out_shape) + tuple(c[3] for c in casts),
        compiler_params=pltpu.CompilerParams(
            dimension_semantics=("arbitrary",) * len(grid), vmem_limit_bytes=VMEM_LIMIT_BYTES),
        name=name,
    )(*prefetch, *operands, *[c[0] for c in casts])
    return outs[:n_out], outs[n_out:]


NORM_TM = 1024


def _norm_kernel(x_ref, g_ref, o_ref):
    o_ref[...] = _rms_norm_bf16(x_ref[...], g_ref[...])


def _norm(xf, g, casts):
    T, D = xf.shape
    tm = NORM_TM
    return _call(
        _norm_kernel, "norm_mix", (T // tm,), (xf, g),
        in_specs=[pl.BlockSpec((tm, D), lambda i: (i, 0)), pl.BlockSpec((1, D), lambda i: (0, 0))],
        out_specs=[pl.BlockSpec((tm, D), lambda i: (i, 0))],
        out_shape=[jax.ShapeDtypeStruct((T, D), BF16)],
        casts=casts)


PROJ_TM = 512


def _rope(t, cos, sin_lo, sin_hi):
    return (t * cos + pltpu.roll(t, HEAD_DIM - ROPE_DIM // 2, 1) * sin_lo
            + pltpu.roll(t, ROPE_DIM // 2, 1) * sin_hi)


def _qkv_kernel(h_ref, w_ref, cos_ref, slo_ref, shi_ref, q_ref, k_ref, vt_ref, km_ref):
    tm = PROJ_TM
    h = h_ref[...]
    cos, slo, shi = cos_ref[...], slo_ref[...], shi_ref[...]
    heads_per_chunk = MXU_COLS // HEAD_DIM
    for c in range(ATTN_WIDTH // MXU_COLS):
        csl = slice(c * MXU_COLS, (c + 1) * MXU_COLS)
        qc = jnp.dot(h, w_ref[:, csl], preferred_element_type=F32)
        kc = jnp.dot(h, w_ref[:, ATTN_WIDTH + c * MXU_COLS:ATTN_WIDTH + (c + 1) * MXU_COLS],
                     preferred_element_type=F32)
        vc = jnp.dot(h, w_ref[:, 2 * ATTN_WIDTH + c * MXU_COLS:2 * ATTN_WIDTH + (c + 1) * MXU_COLS],
                     preferred_element_type=F32)
        vt_ref[0, csl, :] = vc.T.astype(BF16)
        for hh in range(heads_per_chunk):
            sl = slice(hh * HEAD_DIM, (hh + 1) * HEAD_DIM)
            osl = slice(c * MXU_COLS + hh * HEAD_DIM, c * MXU_COLS + (hh + 1) * HEAD_DIM)
            q_ref[:, osl] = (_rope(qc[:, sl], cos, slo, shi) * (ATTN_SCALE * LOG2E)).astype(BF16)
            kf = _rope(kc[:, sl], cos, slo, shi)
            k_ref[:, osl] = kf.astype(BF16)
            for r in range(tm // MOBA_BLOCK):
                km_ref[r, :, osl] = jnp.mean(kf[r * MOBA_BLOCK:(r + 1) * MOBA_BLOCK], axis=0, keepdims=True)


def _qkv_proj(h, w_qkv, cos, slo, shi, seq, casts):
    T, D = h.shape
    tm = PROJ_TM
    tpb = seq // tm
    row = lambda i: (i, 0)
    pos = lambda i: (i % tpb, 0)
    return _call(
        _qkv_kernel, "qkv_proj", (T // tm,), (h, w_qkv, cos, slo, shi),
        in_specs=[
            pl.BlockSpec((tm, D), row),
            _resident((D, 3 * ATTN_WIDTH)),
            pl.BlockSpec((tm, HEAD_DIM), pos),
            pl.BlockSpec((tm, HEAD_DIM), pos),
            pl.BlockSpec((tm, HEAD_DIM), pos),
        ],
        out_specs=[
            pl.BlockSpec((tm, ATTN_WIDTH), row),
            pl.BlockSpec((tm, ATTN_WIDTH), row),
            pl.BlockSpec((1, ATTN_WIDTH, tm), lambda i: (i, 0, 0)),
            pl.BlockSpec((tm // MOBA_BLOCK, 1, ATTN_WIDTH), lambda i: (i, 0, 0)),
        ],
        out_shape=[
            jax.ShapeDtypeStruct((T, ATTN_WIDTH), BF16),
            jax.ShapeDtypeStruct((T, ATTN_WIDTH), BF16),
            jax.ShapeDtypeStruct((T // tm, ATTN_WIDTH, tm), BF16),
            jax.ShapeDtypeStruct((T // MOBA_BLOCK, 1, ATTN_WIDTH), F32),
        ],
        casts=casts)


def _conv_kernel(tiles_per_batch, h_ref, w_ref, cw_ref, z_ref, u_scr):
    tm = PROJ_TM
    i = pl.program_id(0)

    @pl.when(i % tiles_per_batch == 0)
    def _():
        u_scr[0:SUBLANES, :] = jnp.zeros((SUBLANES, CONV_WIDTH), F32)

    h = h_ref[...]
    for c in range(CONV_WIDTH // MXU_COLS):
        csl = slice(c * MXU_COLS, (c + 1) * MXU_COLS)
        cx = jnp.dot(h, w_ref[:, csl], preferred_element_type=F32)
        cb = jnp.dot(h, w_ref[:, CONV_WIDTH + c * MXU_COLS:CONV_WIDTH + (c + 1) * MXU_COLS],
                     preferred_element_type=F32)
        cc = jnp.dot(h, w_ref[:, 2 * CONV_WIDTH + c * MXU_COLS:2 * CONV_WIDTH + (c + 1) * MXU_COLS],
                     preferred_element_type=F32)
        u = cc * cx
        u_scr[SUBLANES:tm + SUBLANES, csl] = u
        conv = (cw_ref[0:1, csl] * u_scr[SUBLANES - 2:tm + SUBLANES - 2, csl]
                + cw_ref[1:2, csl] * u_scr[SUBLANES - 1:tm + SUBLANES - 1, csl]
                + cw_ref[2:3, csl] * u)
        z_ref[:, csl] = (cb * conv).astype(BF16)
        u_scr[0:SUBLANES, csl] = u[tm - SUBLANES:tm]


def _conv_proj(h, w_conv, conv_w, seq, casts):
    T, D = h.shape
    tm = PROJ_TM
    return _call(
        functools.partial(_conv_kernel, seq // tm), "conv_proj", (T // tm,), (h, w_conv, conv_w),
        in_specs=[
            pl.BlockSpec((tm, D), lambda i: (i, 0)),
            _resident((D, 3 * CONV_WIDTH)),
            pl.BlockSpec((CONV_K, CONV_WIDTH), lambda i: (0, 0)),
        ],
        out_specs=[pl.BlockSpec((tm, CONV_WIDTH), lambda i: (i, 0))],
        out_shape=[jax.ShapeDtypeStruct((T, CONV_WIDTH), BF16)],
        scratch_shapes=[pltpu.VMEM((tm + SUBLANES, CONV_WIDTH), F32)],
        casts=casts)


def _gate_kernel(h_ref, wa_ref, wc_ref, o_ref):
    h = h_ref[...]
    for g, w_ref in enumerate((wa_ref, wc_ref)):
        n = w_ref.shape[1]
        for c in range(n // MXU_COLS):
            csl = slice(c * MXU_COLS, (c + 1) * MXU_COLS)
            o_ref[:, g * n + c * MXU_COLS:g * n + (c + 1) * MXU_COLS] = jax.nn.sigmoid(
                jnp.dot(h, w_ref[:, csl], preferred_element_type=F32)).astype(BF16)


def _gate_proj(h, w_ga, w_gc, casts):
    T, D = h.shape
    tm = PROJ_TM
    return _call(
        _gate_kernel, "gate_proj", (T // tm,), (h, w_ga, w_gc),
        in_specs=[pl.BlockSpec((tm, D), lambda i: (i, 0)), _resident((D, D)), _resident((D, D))],
        out_specs=[pl.BlockSpec((tm, 2 * D), lambda i: (i, 0))],
        out_shape=[jax.ShapeDtypeStruct((T, 2 * D), BF16)],
        casts=casts)


MOBA_PAIR = 2
MOBA_UNROLL = 8
MASK_NEG = -1e30
ONES_ROWS = BF16_SUBLANES


def _moba_items(nb):
    qbs, prs = [], []
    for qb in range(nb):
        for pr in range((qb + MOBA_PAIR - 1) // MOBA_PAIR):
            qbs.append(qb)
            prs.append(pr)
    assert len(qbs) % MOBA_UNROLL == 0
    pad = 2 * MOBA_UNROLL
    tqb = np.asarray([nb] * pad + qbs + [nb] * pad, np.int32)
    tpr = np.asarray([0] * pad + prs + [0] * pad, np.int32)
    return tqb, tpr, len(qbs) // MOBA_UNROLL


def _moba_kernel(nb, n_trips, tqb_ref, tpr_ref, q_ref, k_ref, vt_ref, km_ref, hot_ref, o_ref,
                 qbias_scr, s_scr, p_scr, acc_scr, m_scr):
    blk = MOBA_BLOCK
    span = MOBA_PAIR * blk
    ones = jnp.ones((ONES_ROWS, span), BF16)

    km = km_ref[...]
    km_hi = km.astype(BF16)
    km_lo = (km - km_hi.astype(F32)).astype(BF16)
    bidx = lax.broadcasted_iota(jnp.int32, (nb, blk), 0)
    causal = (lax.broadcasted_iota(jnp.int32, (blk, blk), 0)
              <= lax.broadcasted_iota(jnp.int32, (blk, blk), 1))

    def choose_and_score(qb):
        rows = slice(qb * blk, (qb + 1) * blk)
        q = q_ref[rows, :]
        gate = (lax.dot_general(km_hi, q, NT_DIMS, preferred_element_type=F32)
                + lax.dot_general(km_lo, q, NT_DIMS, preferred_element_type=F32))
        g = jnp.where(bidx < qb, gate, NEG_INF)
        bias = jnp.full((nb, blk), MASK_NEG, F32)
        for _ in range(MOBA_TOPK):
            top = jnp.max(g, axis=0, keepdims=True)
            first = jnp.min(jnp.where(g == top, bidx, nb), axis=0, keepdims=True)
            hit = bidx == first
            bias = jnp.where(hit & (top > NEG_INF), 0.0, bias)
            g = jnp.where(hit, NEG_INF, g)
        bias = jnp.concatenate([bias, jnp.zeros((HEAD_DIM - nb, blk), F32)], axis=0)
        qbias_scr[rows, :] = bias.T.astype(BF16)
        return lax.dot_general(k_ref[rows, :], q, NT_DIMS, preferred_element_type=F32)

    def own_block(qb, s):
        s = jnp.where(causal, s, NEG_INF)
        m = jnp.max(s, axis=0, keepdims=True)
        p = jnp.exp2(s - m).astype(BF16)
        col = (qb % MOBA_PAIR) * blk
        vt = jnp.concatenate([vt_ref[qb // MOBA_PAIR, :, col:col + blk], ones[:, :blk]], axis=0)
        acc_scr[qb] = jnp.dot(vt, p, preferred_element_type=F32)
        m_scr[qb] = m

    own_lag = 2
    scored = [choose_and_score(qb) for qb in range(min(own_lag, nb))]
    for qb in range(nb):
        if qb + own_lag < nb:
            scored.append(choose_and_score(qb + own_lag))
        own_block(qb, scored[qb])

    acc_scr[nb] = jnp.zeros(acc_scr.shape[1:], F32)
    m_scr[nb] = jnp.zeros((1, blk), F32)
    s_scr[...] = jnp.zeros(s_scr.shape, F32)
    p_scr[...] = jnp.zeros(p_scr.shape, BF16)

    def scores(idx, slot):
        qb, pr = tqb_ref[idx], tpr_ref[idx]
        qrow = pl.multiple_of(jnp.minimum(qb, nb - 1) * blk, blk)
        krow = pl.multiple_of(pr * span, span)
        k_ext = jnp.concatenate([k_ref[pl.ds(krow, span), :], hot_ref[pl.ds(krow, span), :]], axis=1)
        q_ext = jnp.concatenate([q_ref[pl.ds(qrow, blk), :], qbias_scr[pl.ds(qrow, blk), :]], axis=1)
        s_scr[slot] = lax.dot_general(k_ext, q_ext, NT_DIMS, preferred_element_type=F32)

    def softmax(idx, slot, m):
        qb, pr = tqb_ref[idx], tpr_ref[idx]
        m = jnp.where(pr == 0, m_scr[qb], m)
        s = s_scr[slot]
        m_new = jnp.maximum(m, jnp.max(s, axis=0, keepdims=True))
        p_scr[slot] = jnp.exp2(s - m_new).astype(BF16)
        return m_new, jnp.exp2(m - m_new)

    def pv(idx, slot, alpha):
        qb, pr = tqb_ref[idx], tpr_ref[idx]
        vt = jnp.concatenate([vt_ref[pr], ones], axis=0)
        acc_scr[qb] = alpha * acc_scr[qb] + jnp.dot(vt, p_scr[slot], preferred_element_type=F32)

    def trip(t, carry):
        m, alphas = carry
        base = t * MOBA_UNROLL
        new_alphas = []
        for i in range(MOBA_UNROLL):
            pv(base + i, i, alphas[i])
            m, alpha = softmax(base + MOBA_UNROLL + i, i, m)
            new_alphas.append(alpha)
            scores(base + 2 * MOBA_UNROLL + i, i)
        return m, tuple(new_alphas)

    row0 = jnp.zeros((1, blk), F32)
    lax.fori_loop(0, n_trips + 2, trip, (row0, (row0,) * MOBA_UNROLL))

    for qb in range(nb):
        acc = acc_scr[qb]
        inv = 1.0 / acc[HEAD_DIM:HEAD_DIM + 1]
        o_ref[qb * blk:(qb + 1) * blk, :] = (acc[:HEAD_DIM] * inv).T.astype(BF16)


def _moba_attention(q, k, vt, kmean, batch, seq, casts):
    T = q.shape[0]
    blk = MOBA_BLOCK
    nb = seq // blk
    span = MOBA_PAIR * blk
    assert nb % MOBA_PAIR == 0 and nb <= HEAD_DIM and vt.shape == (T // span, ATTN_WIDTH, span)
    tqb, tpr, n_trips = _moba_items(nb)
    hot = jnp.asarray(np.arange(seq)[:, None] // blk == np.arange(HEAD_DIM)[None, :], BF16)
    head = lambda b, h, *_: (b, h)
    return _call(
        functools.partial(_moba_kernel, nb, n_trips), "moba_attention", (batch, N_ATTN_HEADS),
        (q, k, vt, kmean, hot),
        in_specs=[
            pl.BlockSpec((seq, HEAD_DIM), head),
            pl.BlockSpec((seq, HEAD_DIM), head),
            pl.BlockSpec((seq // span, HEAD_DIM, span), lambda b, h, *_: (b, h, 0)),
            pl.BlockSpec((nb, HEAD_DIM), head),
            _resident((seq, HEAD_DIM)),
        ],
        out_specs=[pl.BlockSpec((seq, HEAD_DIM), head)],
        out_shape=[jax.ShapeDtypeStruct((T, ATTN_WIDTH), BF16)],
        scratch_shapes=[
            pltpu.VMEM((seq, HEAD_DIM), BF16),
            pltpu.VMEM((MOBA_UNROLL, span, blk), F32),
            pltpu.VMEM((MOBA_UNROLL, span, blk), BF16),
            pltpu.VMEM((nb + 1, HEAD_DIM + ONES_ROWS, blk), F32),
            pltpu.VMEM((nb + 1, 1, blk), F32),
        ],
        casts=casts,
        prefetch=(jnp.asarray(tqb), jnp.asarray(tpr)))


MIX_TM = 256


def _mix_kernel(a_ref, z_ref, sga_ref, sgc_ref, x_ref, wa_ref, wc_ref, wm_ref, o_ref):
    y_attn = jnp.dot(a_ref[...], wa_ref[...], preferred_element_type=F32)
    y_conv = jnp.dot(z_ref[...], wc_ref[...], preferred_element_type=F32)
    merged = sga_ref[...].astype(F32) * y_attn + sgc_ref[...].astype(F32) * y_conv
    o_ref[...] = x_ref[...] + jnp.dot(merged.astype(BF16), wm_ref[...], preferred_element_type=F32)


def _mix(attn, z, gates, xf, w_attn_out, w_conv_out, w_mix_out, casts):
    T, D = xf.shape
    tm = MIX_TM
    row = lambda i: (i, 0)
    return _call(
        _mix_kernel, "mix", (T // tm,), (attn, z, gates, gates, xf, w_attn_out, w_conv_out, w_mix_out),
        in_specs=[
            pl.BlockSpec((tm, ATTN_WIDTH), row),
            pl.BlockSpec((tm, CONV_WIDTH), row),
            pl.BlockSpec((tm, D), row),
            pl.BlockSpec((tm, D), lambda i: (i, 1)),
            pl.BlockSpec((tm, D), row),
            _resident((ATTN_WIDTH, D)),
            _resident((CONV_WIDTH, D)),
            _resident((D, D)),
        ],
        out_specs=[pl.BlockSpec((tm, D), row)],
        out_shape=[jax.ShapeDtypeStruct((T, D), F32)],
        casts=casts)


def _mem_kv_kernel(mem_ref, g_ref, w_ref, o_ref):
    o_ref[...] = jnp.dot(_rms_norm_bf16(mem_ref[...], g_ref[...]), w_ref[...],
                         preferred_element_type=F32).astype(BF16)


def _mem_kv(memf, g, wkv):
    R, D = memf.shape
    N = wkv.shape[1]
    return _call(
        _mem_kv_kernel, "mem_kv", (1,), (memf, g, wkv),
        in_specs=[pl.BlockSpec((R, D), lambda i: (0, 0)),
                  pl.BlockSpec((1, D), lambda i: (0, 0)),
                  pl.BlockSpec((D, N), lambda i: (0, 0))],
        out_specs=[pl.BlockSpec((R, N), lambda i: (0, 0))],
        out_shape=[jax.ShapeDtypeStruct((R, N), BF16)])


XATTN_TM = 512


def _xattn_kernel(x_ref, g_ref, wq_ref, kv_ref, wo_ref, o_ref):
    xf = x_ref[...]
    hq = jnp.dot(_rms_norm_bf16(xf, g_ref[...]), wq_ref[...], preferred_element_type=F32)
    hq = (hq * (ATTN_SCALE * LOG2E)).astype(BF16)
    outs = []
    for hh in range(N_XATTN_HEADS):
        sl = slice(hh * HEAD_DIM, (hh + 1) * HEAD_DIM)
        mk = kv_ref[:, sl]
        mv = kv_ref[:, XATTN_WIDTH + hh * HEAD_DIM:XATTN_WIDTH + (hh + 1) * HEAD_DIM]
        s = lax.dot_general(hq[:, sl], mk, NT_DIMS, preferred_element_type=F32)
        p = jnp.exp2(s - jnp.max(s, axis=-1, keepdims=True))
        inv = 1.0 / jnp.sum(p, axis=-1, keepdims=True)
        outs.append((jnp.dot(p.astype(BF16), mv, preferred_element_type=F32) * inv).astype(BF16))
    att = jnp.concatenate(outs, axis=-1)
    o_ref[...] = xf + jnp.dot(att, wo_ref[...], preferred_element_type=F32)


def _xattn(x1, g, wq, kv, wo, seq, n_mem):
    T, D = x1.shape
    tm = XATTN_TM
    tpb = seq // tm
    return _call(
        _xattn_kernel, "xattn", (T // tm,), (x1, g, wq, kv, wo),
        in_specs=[
            pl.BlockSpec((tm, D), lambda i: (i, 0)),
            pl.BlockSpec((1, D), lambda i: (0, 0)),
            _resident((D, XATTN_WIDTH)),
            pl.BlockSpec((n_mem, 2 * XATTN_WIDTH), lambda i: (i // tpb, 0)),
            _resident((XATTN_WIDTH, D)),
        ],
        out_specs=[pl.BlockSpec((tm, D), lambda i: (i, 0))],
        out_shape=[jax.ShapeDtypeStruct((T, D), F32)])


MLP_TM = 512
MLP_TF = 1024


def _mlp_kernel(x_ref, g_ref, wu_ref, wd_ref, gf_ref, o_ref, h_scr, acc_scr):
    f = pl.program_id(1)

    @pl.when(f == 0)
    def _():
        xf = x_ref[...]
        h_scr[...] = _rms_norm_bf16(xf, g_ref[...])
        acc_scr[...] = xf

    u = jnp.maximum(jnp.dot(h_scr[...], wu_ref[...], preferred_element_type=F32), 0.0)
    acc_scr[...] += jnp.dot((u * u).astype(BF16), wd_ref[...], preferred_element_type=F32)

    @pl.when(f == pl.num_programs(1) - 1)
    def _():
        y = acc_scr[...]
        ms = jnp.mean(y * y, axis=-1, keepdims=True)
        o_ref[...] = y * lax.rsqrt(ms + NORM_EPS) * gf_ref[...]


def _mlp(x2, g, w_up, w_down, g_final):
    T, D = x2.shape
    FF = w_up.shape[1]
    tm, tf = MLP_TM, MLP_TF
    return _call(
        _mlp_kernel, "mlp", (T // tm, FF // tf), (x2, g, w_up, w_down, g_final),
        in_specs=[
            pl.BlockSpec((tm, D), lambda i, f: (i, 0)),
            pl.BlockSpec((1, D), lambda i, f: (0, 0)),
            pl.BlockSpec((D, tf), lambda i, f: (0, f)),
            pl.BlockSpec((tf, D), lambda i, f: (f, 0)),
            pl.BlockSpec((1, D), lambda i, f: (0, 0)),
        ],
        out_specs=[pl.BlockSpec((tm, D), lambda i, f: (i, 0))],
        out_shape=[jax.ShapeDtypeStruct((T, D), F32)],
        scratch_shapes=[pltpu.VMEM((tm, D), BF16), pltpu.VMEM((tm, D), F32)])


def _rope_tables(seq):
    half = ROPE_DIM // 2
    inv_freq = ROPE_THETA ** (-jnp.arange(half, dtype=F32) / half)
    ang = jnp.arange(seq).astype(F32)[:, None] * inv_freq[None, :]
    cos, sin = jnp.cos(ang), jnp.sin(ang)
    pad = HEAD_DIM - ROPE_DIM
    cos_t = jnp.concatenate([cos, cos, jnp.ones((seq, pad), F32)], axis=-1)
    sin_lo = jnp.concatenate([-sin, jnp.zeros((seq, half + pad), F32)], axis=-1)
    sin_hi = jnp.concatenate([jnp.zeros((seq, half), F32), sin, jnp.zeros((seq, pad), F32)], axis=-1)
    return cos_t, sin_lo, sin_hi


def _layer(xf, memf, batch, seq, n_mem, norm_mix, w_in, conv_w, w_attn_out, w_conv_out, w_mix_out,
           norm_xattn, norm_mem, wq_x, wkv_x, wo_x, norm_mlp, w_up, w_down, g_out):
    T, D = xf.shape
    row = lambda g: g.reshape(1, D).astype(F32)
    cos_t, sin_lo, sin_hi = _rope_tables(seq)
    n_qkv, n_conv = 3 * ATTN_WIDTH, 3 * CONV_WIDTH
    assert n_qkv == n_conv and (n_qkv + n_conv) % D == 0
    proj_steps = T // PROJ_TM
    head_step = lambda b, h, *_: b * N_ATTN_HEADS + h

    (h,), (w_qkv,) = _norm(xf, row(norm_mix), [_cast_job(w_in, T // NORM_TM, n_qkv, 0)])
    (q, k, vt, kmean), (w_conv, w_ga, w_gc) = _qkv_proj(
        h, w_qkv, cos_t, sin_lo, sin_hi, seq,
        [_cast_job(w_in, proj_steps, n_conv, 1),
         _cast_job(w_in, proj_steps, D, (n_qkv + n_conv) // D),
         _cast_job(w_in, proj_steps, D, (n_qkv + n_conv) // D + 1)])
    (z,), (wa, wc, wm) = _conv_proj(
        h, w_conv, conv_w, seq,
        [_cast_job(w_attn_out, proj_steps), _cast_job(w_conv_out, proj_steps), _cast_job(w_mix_out, proj_steps)])
    (gates,), (wq, wkv, wo) = _gate_proj(
        h, w_ga, w_gc, [_cast_job(wq_x, proj_steps), _cast_job(wkv_x, proj_steps), _cast_job(wo_x, proj_steps)])
    kmean = kmean.reshape(T // MOBA_BLOCK, ATTN_WIDTH)
    (attn,), (wu,) = _moba_attention(
        q, k, vt, kmean, batch, seq, [_cast_job(w_up, batch * N_ATTN_HEADS, step_of=head_step)])
    (x1,), (wd,) = _mix(attn, z, gates, xf, wa, wc, wm, [_cast_job(w_down, T // MIX_TM)])
    (kv,), _ = _mem_kv(memf, row(norm_mem), wkv)
    (x2,), _ = _xattn(x1, row(norm_xattn), wq, kv, wo, seq, n_mem)
    (out,), _ = _mlp(x2, row(norm_mlp), wu, wd, g_out)
    return out


def kernel(x, mem, norm_mix, w_in, conv_w, w_attn_out, w_conv_out, w_mix_out, norm_xattn, norm_mem,
           wq_x, wkv_x, wo_x, norm_mlp, w_up, w_down, norm_final):
    B, S, D = x.shape
    n_mem = mem.shape[1]
    depth = w_in.shape[0]
    assert depth == 1, "the final rmsnorm is fused into the last layer's MLP kernel"
    xf = x.reshape(B * S, D)
    memf = mem.reshape(B * n_mem, D)
    out = _layer(xf, memf, B, S, n_mem, norm_mix[0], w_in[0], conv_w[0], w_attn_out[0], w_conv_out[0],
                 w_mix_out[0], norm_xattn[0], norm_mem[0], wq_x[0], wkv_x[0], wo_x[0], norm_mlp[0],
                 w_up[0], w_down[0], norm_final.reshape(1, D).astype(F32))
    return out.reshape(B, S, D)
```

```python
import functools
import math

import jax
import jax.numpy as jnp
import numpy as np
from jax import lax
from jax.experimental import pallas as pl
from jax.experimental.pallas import tpu as pltpu

F32 = jnp.float32
BF16 = jnp.bfloat16

HEAD_DIM = 128
N_ATTN_HEADS = 8
ATTN_WIDTH = N_ATTN_HEADS * HEAD_DIM
CONV_WIDTH = 1024
CONV_K = 3
MOBA_BLOCK = 256
MOBA_TOPK = 3
ROPE_THETA = 500000.0
ROPE_DIM = HEAD_DIM // 4
N_XATTN_HEADS = 4
XATTN_WIDTH = N_XATTN_HEADS * HEAD_DIM
NORM_EPS = 1e-6

LOG2E = math.log2(math.e)
ATTN_SCALE = HEAD_DIM ** -0.5
NEG_INF = float("-inf")

VMEM_LIMIT_BYTES = 52 * 1024 * 1024
SUBLANES = 8
BF16_SUBLANES = 16
MXU_COLS = 256

NT_DIMS = (((1,), (1,)), ((), ()))


def _rms_norm_bf16(xf, g):
    ms = jnp.mean(xf * xf, axis=-1, keepdims=True)
    return (xf * lax.rsqrt(ms + NORM_EPS) * g).astype(BF16)


def _resident(shape):
    return pl.BlockSpec(shape, lambda *_: (0,) * len(shape), pipeline_mode=pl.Buffered(1))


def _cast_job(w, steps, n_cols=None, col_block=0, step_of=lambda i, *_: i):
    n_rows = w.shape[0]
    n_cols = w.shape[1] if n_cols is None else n_cols
    rows = n_rows // steps
    assert rows * steps == n_rows and rows % BF16_SUBLANES == 0
    in_spec = pl.BlockSpec((rows, n_cols), lambda *g: (step_of(*g), col_block))
    out_spec = pl.BlockSpec((rows, n_cols), lambda *g: (step_of(*g), 0))
    return w, in_spec, out_spec, jax.ShapeDtypeStruct((n_rows, n_cols), BF16)


def _call(body, name, grid, operands, in_specs, out_specs, out_shape, scratch_shapes=(), casts=(),
          prefetch=()):
    n_in = len(prefetch) + len(operands)
    n_out = len(out_shape)
    n_cast = len(casts)

    def kern(*refs):
        ins, refs = refs[:n_in], refs[n_in:]
        cast_src, refs = refs[:n_cast], refs[n_cast:]
        outs, refs = refs[:n_out], refs[n_out:]
        cast_dst, scratch = refs[:n_cast], refs[n_cast:]
        body(*ins, *outs, *scratch)
        for src, dst in zip(cast_src, cast_dst):
            dst[...] = src[...].astype(BF16)

    grid_spec = pltpu.PrefetchScalarGridSpec(
        num_scalar_prefetch=len(prefetch),
        grid=grid,
        in_specs=list(in_specs) + [c[1] for c in casts],
        out_specs=tuple(out_specs) + tuple(c[2] for c in casts),
        scratch_shapes=list(scratch_shapes),
    )
    outs = pl.pallas_call(
        kern,
        grid_spec=grid_spec,
        out_shape=tuple(out_shape) + tuple(c[3] for c in casts),
        compiler_params=pltpu.CompilerParams(
            dimension_semantics=("arbitrary",) * len(grid), vmem_limit_bytes=VMEM_LIMIT_BYTES),
        name=name,
    )(*prefetch, *operands, *[c[0] for c in casts])
    return outs[:n_out], outs[n_out:]


NORM_TM = 1024


def _norm_kernel(x_ref, g_ref, o_ref):
    o_ref[...] = _rms_norm_bf16(x_ref[...], g_ref[...])


def _norm(xf, g, casts):
    T, D = xf.shape
    tm = NORM_TM
    return _call(
        _norm_kernel, "norm_mix", (T // tm,), (xf, g),
        in_specs=[pl.BlockSpec((tm, D), lambda i: (i, 0)), pl.BlockSpec((1, D), lambda i: (0, 0))],
        out_specs=[pl.BlockSpec((tm, D), lambda i: (i, 0))],
        out_shape=[jax.ShapeDtypeStruct((T, D), BF16)],
        casts=casts)


PROJ_TM = 512


def _rope(t, cos, sin_lo, sin_hi):
    return (t * cos + pltpu.roll(t, HEAD_DIM - ROPE_DIM // 2, 1) * sin_lo
            + pltpu.roll(t, ROPE_DIM // 2, 1) * sin_hi)


def _qkv_kernel(h_ref, w_ref, cos_ref, slo_ref, shi_ref, qt_ref, k_ref, vt_ref, km_ref):
    tm = PROJ_TM
    h = h_ref[...]
    cos, slo, shi = cos_ref[...], slo_ref[...], shi_ref[...]
    heads_per_chunk = MXU_COLS // HEAD_DIM
    for c in range(ATTN_WIDTH // MXU_COLS):
        csl = slice(c * MXU_COLS, (c + 1) * MXU_COLS)
        qc = jnp.dot(h, w_ref[:, csl], preferred_element_type=F32)
        kc = jnp.dot(h, w_ref[:, ATTN_WIDTH + c * MXU_COLS:ATTN_WIDTH + (c + 1) * MXU_COLS],
                     preferred_element_type=F32)
        vc = jnp.dot(h, w_ref[:, 2 * ATTN_WIDTH + c * MXU_COLS:2 * ATTN_WIDTH + (c + 1) * MXU_COLS],
                     preferred_element_type=F32)
        vt_ref[0, csl, :] = vc.T.astype(BF16)
        for hh in range(heads_per_chunk):
            sl = slice(hh * HEAD_DIM, (hh + 1) * HEAD_DIM)
            osl = slice(c * MXU_COLS + hh * HEAD_DIM, c * MXU_COLS + (hh + 1) * HEAD_DIM)
            qf = _rope(qc[:, sl], cos, slo, shi) * (ATTN_SCALE * LOG2E)
            kf = _rope(kc[:, sl], cos, slo, shi)
            k_ref[:, osl] = kf.astype(BF16)
            for r in range(tm // MOBA_BLOCK):
                rsl = slice(r * MOBA_BLOCK, (r + 1) * MOBA_BLOCK)
                qt_ref[r, osl, :] = qf[rsl].T.astype(BF16)
                km_ref[r, :, osl] = jnp.mean(kf[rsl], axis=0, keepdims=True)


def _qkv_proj(h, w_qkv, cos, slo, shi, seq, casts):
    T, D = h.shape
    tm = PROJ_TM
    tpb = seq // tm
    row = lambda i: (i, 0)
    pos = lambda i: (i % tpb, 0)
    return _call(
        _qkv_kernel, "qkv_proj", (T // tm,), (h, w_qkv, cos, slo, shi),
        in_specs=[
            pl.BlockSpec((tm, D), row),
            _resident((D, 3 * ATTN_WIDTH)),
            pl.BlockSpec((tm, HEAD_DIM), pos),
            pl.BlockSpec((tm, HEAD_DIM), pos),
            pl.BlockSpec((tm, HEAD_DIM), pos),
        ],
        out_specs=[
            pl.BlockSpec((tm // MOBA_BLOCK, ATTN_WIDTH, MOBA_BLOCK), lambda i: (i, 0, 0)),
            pl.BlockSpec((tm, ATTN_WIDTH), row),
            pl.BlockSpec((1, ATTN_WIDTH, tm), lambda i: (i, 0, 0)),
            pl.BlockSpec((tm // MOBA_BLOCK, 1, ATTN_WIDTH), lambda i: (i, 0, 0)),
        ],
        out_shape=[
            jax.ShapeDtypeStruct((T // MOBA_BLOCK, ATTN_WIDTH, MOBA_BLOCK), BF16),
            jax.ShapeDtypeStruct((T, ATTN_WIDTH), BF16),
            jax.ShapeDtypeStruct((T // tm, ATTN_WIDTH, tm), BF16),
            jax.ShapeDtypeStruct((T // MOBA_BLOCK, 1, ATTN_WIDTH), F32),
        ],
        casts=casts)


def _conv_kernel(tiles_per_batch, h_ref, w_ref, cw_ref, z_ref, u_scr):
    tm = PROJ_TM
    i = pl.program_id(0)

    @pl.when(i % tiles_per_batch == 0)
    def _():
        u_scr[0:SUBLANES, :] = jnp.zeros((SUBLANES, CONV_WIDTH), F32)

    h = h_ref[...]
    for c in range(CONV_WIDTH // MXU_COLS):
        csl = slice(c * MXU_COLS, (c + 1) * MXU_COLS)
        cx = jnp.dot(h, w_ref[:, csl], preferred_element_type=F32)
        cb = jnp.dot(h, w_ref[:, CONV_WIDTH + c * MXU_COLS:CONV_WIDTH + (c + 1) * MXU_COLS],
                     preferred_element_type=F32)
        cc = jnp.dot(h, w_ref[:, 2 * CONV_WIDTH + c * MXU_COLS:2 * CONV_WIDTH + (c + 1) * MXU_COLS],
                     preferred_element_type=F32)
        u = cc * cx
        u_scr[SUBLANES:tm + SUBLANES, csl] = u
        conv = (cw_ref[0:1, csl] * u_scr[SUBLANES - 2:tm + SUBLANES - 2, csl]
                + cw_ref[1:2, csl] * u_scr[SUBLANES - 1:tm + SUBLANES - 1, csl]
                + cw_ref[2:3, csl] * u)
        z_ref[:, csl] = (cb * conv).astype(BF16)
        u_scr[0:SUBLANES, csl] = u[tm - SUBLANES:tm]


def _conv_proj(h, w_conv, conv_w, seq, casts):
    T, D = h.shape
    tm = PROJ_TM
    return _call(
        functools.partial(_conv_kernel, seq // tm), "conv_proj", (T // tm,), (h, w_conv, conv_w),
        in_specs=[
            pl.BlockSpec((tm, D), lambda i: (i, 0)),
            _resident((D, 3 * CONV_WIDTH)),
            pl.BlockSpec((CONV_K, CONV_WIDTH), lambda i: (0, 0)),
        ],
        out_specs=[pl.BlockSpec((tm, CONV_WIDTH), lambda i: (i, 0))],
        out_shape=[jax.ShapeDtypeStruct((T, CONV_WIDTH), BF16)],
        scratch_shapes=[pltpu.VMEM((tm + SUBLANES, CONV_WIDTH), F32)],
        casts=casts)


def _gate_kernel(h_ref, wa_ref, wc_ref, o_ref):
    h = h_ref[...]
    for g, w_ref in enumerate((wa_ref, wc_ref)):
        n = w_ref.shape[1]
        for c in range(n // MXU_COLS):
            csl = slice(c * MXU_COLS, (c + 1) * MXU_COLS)
            o_ref[:, g * n + c * MXU_COLS:g * n + (c + 1) * MXU_COLS] = jax.nn.sigmoid(
                jnp.dot(h, w_ref[:, csl], preferred_element_type=F32)).astype(BF16)


def _gate_proj(h, w_ga, w_gc, casts):
    T, D = h.shape
    tm = PROJ_TM
    return _call(
        _gate_kernel, "gate_proj", (T // tm,), (h, w_ga, w_gc),
        in_specs=[pl.BlockSpec((tm, D), lambda i: (i, 0)), _resident((D, D)), _resident((D, D))],
        out_specs=[pl.BlockSpec((tm, 2 * D), lambda i: (i, 0))],
        out_shape=[jax.ShapeDtypeStruct((T, 2 * D), BF16)],
        casts=casts)


MOBA_PAIR = 2
MOBA_UNROLL = 8
MASK_NEG = -1e30
ONES_ROWS = BF16_SUBLANES


def _moba_items(nb):
    qbs, prs = [], []
    for qb in range(nb):
        for pr in range((qb + MOBA_PAIR - 1) // MOBA_PAIR):
            qbs.append(qb)
            prs.append(pr)
    assert len(qbs) % MOBA_UNROLL == 0
    pad = 2 * MOBA_UNROLL
    tqb = np.asarray([nb] * pad + qbs + [nb] * pad, np.int32)
    tpr = np.asarray([0] * pad + prs + [0] * pad, np.int32)
    return tqb, tpr, len(qbs) // MOBA_UNROLL


def _moba_kernel(nb, n_trips, tqb_ref, tpr_ref, qt_ref, k_ref, vt_ref, km_ref, hot_ref, o_ref,
                 qbias_scr, s_scr, p_scr, acc_scr, m_scr):
    blk = MOBA_BLOCK
    span = MOBA_PAIR * blk
    ones = jnp.ones((ONES_ROWS, span), BF16)

    km = km_ref[...]
    km_hi = km.astype(BF16)
    km_lo = (km - km_hi.astype(F32)).astype(BF16)
    bidx = lax.broadcasted_iota(jnp.int32, (nb, blk), 0)
    causal = (lax.broadcasted_iota(jnp.int32, (blk, blk), 0)
              <= lax.broadcasted_iota(jnp.int32, (blk, blk), 1))

    def choose_and_score(qb):
        qt = qt_ref[qb]
        gate = (jnp.dot(km_hi, qt, preferred_element_type=F32)
                + jnp.dot(km_lo, qt, preferred_element_type=F32))
        g = jnp.where(bidx < qb, gate, NEG_INF)
        bias = jnp.full((nb, blk), MASK_NEG, F32)
        for _ in range(MOBA_TOPK):
            top = jnp.max(g, axis=0, keepdims=True)
            first = jnp.min(jnp.where(g == top, bidx, nb), axis=0, keepdims=True)
            hit = bidx == first
            bias = jnp.where(hit & (top > NEG_INF), 0.0, bias)
            g = jnp.where(hit, NEG_INF, g)
        bias = jnp.concatenate([bias, jnp.zeros((HEAD_DIM - nb, blk), F32)], axis=0)
        qbias_scr[qb] = bias.astype(BF16)
        return jnp.dot(k_ref[qb * blk:(qb + 1) * blk, :], qt, preferred_element_type=F32)

    def own_block(qb, s):
        s = jnp.where(causal, s, NEG_INF)
        m = jnp.max(s, axis=0, keepdims=True)
        p = jnp.exp2(s - m).astype(BF16)
        col = (qb % MOBA_PAIR) * blk
        vt = jnp.concatenate([vt_ref[qb // MOBA_PAIR, :, col:col + blk], ones[:, :blk]], axis=0)
        acc_scr[qb] = jnp.dot(vt, p, preferred_element_type=F32)
        m_scr[qb] = m

    own_lag = 2
    scored = [choose_and_score(qb) for qb in range(min(own_lag, nb))]
    for qb in range(nb):
        if qb + own_lag < nb:
            scored.append(choose_and_score(qb + own_lag))
        own_block(qb, scored[qb])

    acc_scr[nb] = jnp.zeros(acc_scr.shape[1:], F32)
    m_scr[nb] = jnp.zeros((1, blk), F32)
    s_scr[...] = jnp.zeros(s_scr.shape, F32)
    p_scr[...] = jnp.zeros(p_scr.shape, BF16)

    def scores(idx, slot):
        qb, pr = tqb_ref[idx], tpr_ref[idx]
        qb = jnp.minimum(qb, nb - 1)
        krow = pl.multiple_of(pr * span, span)
        k_ext = jnp.concatenate([k_ref[pl.ds(krow, span), :], hot_ref[pl.ds(krow, span), :]], axis=1)
        q_ext = jnp.concatenate([qt_ref[qb], qbias_scr[qb]], axis=0)
        s_scr[slot] = jnp.dot(k_ext, q_ext, preferred_element_type=F32)

    def softmax(idx, slot, m):
        qb, pr = tqb_ref[idx], tpr_ref[idx]
        m = jnp.where(pr == 0, m_scr[qb], m)
        s = s_scr[slot]
        m_new = jnp.maximum(m, jnp.max(s, axis=0, keepdims=True))
        p_scr[slot] = jnp.exp2(s - m_new).astype(BF16)
        return m_new, jnp.exp2(m - m_new)

    def pv(idx, slot, alpha):
        qb, pr = tqb_ref[idx], tpr_ref[idx]
        vt = jnp.concatenate([vt_ref[pr], ones], axis=0)
        acc_scr[qb] = alpha * acc_scr[qb] + jnp.dot(vt, p_scr[slot], preferred_element_type=F32)

    def trip(t, carry):
        m, alphas = carry
        base = t * MOBA_UNROLL
        new_alphas = []
        for i in range(MOBA_UNROLL):
            pv(base + i, i, alphas[i])
            m, alpha = softmax(base + MOBA_UNROLL + i, i, m)
            new_alphas.append(alpha)
            scores(base + 2 * MOBA_UNROLL + i, i)
        return m, tuple(new_alphas)

    row0 = jnp.zeros((1, blk), F32)
    lax.fori_loop(0, n_trips + 2, trip, (row0, (row0,) * MOBA_UNROLL))

    for qb in range(nb):
        acc = acc_scr[qb]
        inv = 1.0 / acc[HEAD_DIM:HEAD_DIM + 1]
        o_ref[qb * blk:(qb + 1) * blk, :] = (acc[:HEAD_DIM] * inv).T.astype(BF16)


def _moba_attention(qt, k, vt, kmean, batch, seq, casts):
    T = k.shape[0]
    blk = MOBA_BLOCK
    nb = seq // blk
    span = MOBA_PAIR * blk
    assert nb % MOBA_PAIR == 0 and nb <= HEAD_DIM
    assert qt.shape == (T // blk, ATTN_WIDTH, blk) and vt.shape == (T // span, ATTN_WIDTH, span)
    tqb, tpr, n_trips = _moba_items(nb)
    hot = jnp.asarray(np.arange(seq)[:, None] // blk == np.arange(HEAD_DIM)[None, :], BF16)
    head = lambda b, h, *_: (b, h)
    return _call(
        functools.partial(_moba_kernel, nb, n_trips), "moba_attention", (batch, N_ATTN_HEADS),
        (qt, k, vt, kmean, hot),
        in_specs=[
            pl.BlockSpec((nb, HEAD_DIM, blk), lambda b, h, *_: (b, h, 0)),
            pl.BlockSpec((seq, HEAD_DIM), head),
            pl.BlockSpec((seq // span, HEAD_DIM, span), lambda b, h, *_: (b, h, 0)),
            pl.BlockSpec((nb, HEAD_DIM), head),
            _resident((seq, HEAD_DIM)),
        ],
        out_specs=[pl.BlockSpec((seq, HEAD_DIM), head)],
        out_shape=[jax.ShapeDtypeStruct((T, ATTN_WIDTH), BF16)],
        scratch_shapes=[
            pltpu.VMEM((nb, HEAD_DIM, blk), BF16),
            pltpu.VMEM((MOBA_UNROLL, span, blk), F32),
            pltpu.VMEM((MOBA_UNROLL, span, blk), BF16),
            pltpu.VMEM((nb + 1, HEAD_DIM + ONES_ROWS, blk), F32),
            pltpu.VMEM((nb + 1, 1, blk), F32),
        ],
        casts=casts,
        prefetch=(jnp.asarray(tqb), jnp.asarray(tpr)))


MIX_TM = 256


def _mix_kernel(a_ref, z_ref, sga_ref, sgc_ref, x_ref, wa_ref, wc_ref, wm_ref, o_ref):
    y_attn = jnp.dot(a_ref[...], wa_ref[...], preferred_element_type=F32)
    y_conv = jnp.dot(z_ref[...], wc_ref[...], preferred_element_type=F32)
    merged = sga_ref[...].astype(F32) * y_attn + sgc_ref[...].astype(F32) * y_conv
    o_ref[...] = x_ref[...] + jnp.dot(merged.astype(BF16), wm_ref[...], preferred_element_type=F32)


def _mix(attn, z, gates, xf, w_attn_out, w_conv_out, w_mix_out, casts):
    T, D = xf.shape
    tm = MIX_TM
    row = lambda i: (i, 0)
    return _call(
        _mix_kernel, "mix", (T // tm,), (attn, z, gates, gates, xf, w_attn_out, w_conv_out, w_mix_out),
        in_specs=[
            pl.BlockSpec((tm, ATTN_WIDTH), row),
            pl.BlockSpec((tm, CONV_WIDTH), row),
            pl.BlockSpec((tm, D), row),
            pl.BlockSpec((tm, D), lambda i: (i, 1)),
            pl.BlockSpec((tm, D), row),
            _resident((ATTN_WIDTH, D)),
            _resident((CONV_WIDTH, D)),
            _resident((D, D)),
        ],
        out_specs=[pl.BlockSpec((tm, D), row)],
        out_shape=[jax.ShapeDtypeStruct((T, D), F32)],
        casts=casts)


def _mem_kv_kernel(mem_ref, g_ref, w_ref, o_ref):
    o_ref[...] = jnp.dot(_rms_norm_bf16(mem_ref[...], g_ref[...]), w_ref[...],
                         preferred_element_type=F32).astype(BF16)


def _mem_kv(memf, g, wkv):
    R, D = memf.shape
    N = wkv.shape[1]
    return _call(
        _mem_kv_kernel, "mem_kv", (1,), (memf, g, wkv),
        in_specs=[pl.BlockSpec((R, D), lambda i: (0, 0)),
                  pl.BlockSpec((1, D), lambda i: (0, 0)),
                  pl.BlockSpec((D, N), lambda i: (0, 0))],
        out_specs=[pl.BlockSpec((R, N), lambda i: (0, 0))],
        out_shape=[jax.ShapeDtypeStruct((R, N), BF16)])


XATTN_TM = 512


def _xattn_kernel(x_ref, g_ref, wq_ref, kv_ref, wo_ref, o_ref):
    xf = x_ref[...]
    hq = jnp.dot(_rms_norm_bf16(xf, g_ref[...]), wq_ref[...], preferred_element_type=F32)
    hq = (hq * (ATTN_SCALE * LOG2E)).astype(BF16)
    outs = []
    for hh in range(N_XATTN_HEADS):
        sl = slice(hh * HEAD_DIM, (hh + 1) * HEAD_DIM)
        mk = kv_ref[:, sl]
        mv = kv_ref[:, XATTN_WIDTH + hh * HEAD_DIM:XATTN_WIDTH + (hh + 1) * HEAD_DIM]
        s = lax.dot_general(hq[:, sl], mk, NT_DIMS, preferred_element_type=F32)
        p = jnp.exp2(s - jnp.max(s, axis=-1, keepdims=True))
        inv = 1.0 / jnp.sum(p, axis=-1, keepdims=True)
        outs.append((jnp.dot(p.astype(BF16), mv, preferred_element_type=F32) * inv).astype(BF16))
    att = jnp.concatenate(outs, axis=-1)
    o_ref[...] = xf + jnp.dot(att, wo_ref[...], preferred_element_type=F32)


def _xattn(x1, g, wq, kv, wo, seq, n_mem):
    T, D = x1.shape
    tm = XATTN_TM
    tpb = seq // tm
    return _call(
        _xattn_kernel, "xattn", (T // tm,), (x1, g, wq, kv, wo),
        in_specs=[
            pl.BlockSpec((tm, D), lambda i: (i, 0)),
            pl.BlockSpec((1, D), lambda i: (0, 0)),
            _resident((D, XATTN_WIDTH)),
            pl.BlockSpec((n_mem, 2 * XATTN_WIDTH), lambda i: (i // tpb, 0)),
            _resident((XATTN_WIDTH, D)),
        ],
        out_specs=[pl.BlockSpec((tm, D), lambda i: (i, 0))],
        out_shape=[jax.ShapeDtypeStruct((T, D), F32)])


MLP_TM = 512
MLP_TF = 1024


def _mlp_kernel(x_ref, g_ref, wu_ref, wd_ref, gf_ref, o_ref, h_scr, acc_scr):
    f = pl.program_id(1)

    @pl.when(f == 0)
    def _():
        xf = x_ref[...]
        h_scr[...] = _rms_norm_bf16(xf, g_ref[...])
        acc_scr[...] = xf

    u = jnp.maximum(jnp.dot(h_scr[...], wu_ref[...], preferred_element_type=F32), 0.0)
    acc_scr[...] += jnp.dot((u * u).astype(BF16), wd_ref[...], preferred_element_type=F32)

    @pl.when(f == pl.num_programs(1) - 1)
    def _():
        y = acc_scr[...]
        ms = jnp.mean(y * y, axis=-1, keepdims=True)
        o_ref[...] = y * lax.rsqrt(ms + NORM_EPS) * gf_ref[...]


def _mlp(x2, g, w_up, w_down, g_final):
    T, D = x2.shape
    FF = w_up.shape[1]
    tm, tf = MLP_TM, MLP_TF
    return _call(
        _mlp_kernel, "mlp", (T // tm, FF // tf), (x2, g, w_up, w_down, g_final),
        in_specs=[
            pl.BlockSpec((tm, D), lambda i, f: (i, 0)),
            pl.BlockSpec((1, D), lambda i, f: (0, 0)),
            pl.BlockSpec((D, tf), lambda i, f: (0, f)),
            pl.BlockSpec((tf, D), lambda i, f: (f, 0)),
            pl.BlockSpec((1, D), lambda i, f: (0, 0)),
        ],
        out_specs=[pl.BlockSpec((tm, D), lambda i, f: (i, 0))],
        out_shape=[jax.ShapeDtypeStruct((T, D), F32)],
        scratch_shapes=[pltpu.VMEM((tm, D), BF16), pltpu.VMEM((tm, D), F32)])


def _rope_tables(seq):
    half = ROPE_DIM // 2
    inv_freq = ROPE_THETA ** (-jnp.arange(half, dtype=F32) / half)
    ang = jnp.arange(seq).astype(F32)[:, None] * inv_freq[None, :]
    cos, sin = jnp.cos(ang), jnp.sin(ang)
    pad = HEAD_DIM - ROPE_DIM
    cos_t = jnp.concatenate([cos, cos, jnp.ones((seq, pad), F32)], axis=-1)
    sin_lo = jnp.concatenate([-sin, jnp.zeros((seq, half + pad), F32)], axis=-1)
    sin_hi = jnp.concatenate([jnp.zeros((seq, half), F32), sin, jnp.zeros((seq, pad), F32)], axis=-1)
    return cos_t, sin_lo, sin_hi


def _layer(xf, memf, batch, seq, n_mem, norm_mix, w_in, conv_w, w_attn_out, w_conv_out, w_mix_out,
           norm_xattn, norm_mem, wq_x, wkv_x, wo_x, norm_mlp, w_up, w_down, g_out):
    T, D = xf.shape
    row = lambda g: g.reshape(1, D).astype(F32)
    cos_t, sin_lo, sin_hi = _rope_tables(seq)
    n_qkv, n_conv = 3 * ATTN_WIDTH, 3 * CONV_WIDTH
    assert n_qkv == n_conv and (n_qkv + n_conv) % D == 0
    proj_steps = T // PROJ_TM
    head_step = lambda b, h, *_: b * N_ATTN_HEADS + h

    (h,), (w_qkv,) = _norm(xf, row(norm_mix), [_cast_job(w_in, T // NORM_TM, n_qkv, 0)])
    (qt, k, vt, kmean), (w_conv, w_ga, w_gc) = _qkv_proj(
        h, w_qkv, cos_t, sin_lo, sin_hi, seq,
        [_cast_job(w_in, proj_steps, n_conv, 1),
         _cast_job(w_in, proj_steps, D, (n_qkv + n_conv) // D),
         _cast_job(w_in, proj_steps, D, (n_qkv + n_conv) // D + 1)])
    (z,), (wa, wc, wm) = _conv_proj(
        h, w_conv, conv_w, seq,
        [_cast_job(w_attn_out, proj_steps), _cast_job(w_conv_out, proj_steps), _cast_job(w_mix_out, proj_steps)])
    (gates,), (wq, wkv, wo) = _gate_proj(
        h, w_ga, w_gc, [_cast_job(wq_x, proj_steps), _cast_job(wkv_x, proj_steps), _cast_job(wo_x, proj_steps)])
    kmean = kmean.reshape(T // MOBA_BLOCK, ATTN_WIDTH)
    (attn,), (wu,) = _moba_attention(
        qt, k, vt, kmean, batch, seq, [_cast_job(w_up, batch * N_ATTN_HEADS, step_of=head_step)])
    (x1,), (wd,) = _mix(attn, z, gates, xf, wa, wc, wm, [_cast_job(w_down, T // MIX_TM)])
    (kv,), _ = _mem_kv(memf, row(norm_mem), wkv)
    (x2,), _ = _xattn(x1, row(norm_xattn), wq, kv, wo, seq, n_mem)
    (out,), _ = _mlp(x2, row(norm_mlp), wu, wd, g_out)
    return out


def kernel(x, mem, norm_mix, w_in, conv_w, w_attn_out, w_conv_out, w_mix_out, norm_xattn, norm_mem,
           wq_x, wkv_x, wo_x, norm_mlp, w_up, w_down, norm_final):
    B, S, D = x.shape
    n_mem = mem.shape[1]
    depth = w_in.shape[0]
    assert depth == 1, "the final rmsnorm is fused into the last layer's MLP kernel"
    xf = x.reshape(B * S, D)
    memf = mem.reshape(B * n_mem, D)
    out = _layer(xf, memf, B, S, n_mem, norm_mix[0], w_in[0], conv_w[0], w_attn_out[0], w_conv_out[0],
                 w_mix_out[0], norm_xattn[0], norm_mem[0], wq_x[0], wkv_x[0], wo_x[0], norm_mlp[0],
                 w_up[0], w_down[0], norm_final.reshape(1, D).astype(F32))
    return out.reshape(B, S, D)
```

```python
import functools
import math

import jax
import jax.numpy as jnp
import numpy as np
from jax import lax
from jax.experimental import pallas as pl
from jax.experimental.pallas import tpu as pltpu

F32 = jnp.float32
BF16 = jnp.bfloat16

HEAD_DIM = 128
N_ATTN_HEADS = 8
ATTN_WIDTH = N_ATTN_HEADS * HEAD_DIM
CONV_WIDTH = 1024
CONV_K = 3
MOBA_BLOCK = 256
MOBA_TOPK = 3
ROPE_THETA = 500000.0
ROPE_DIM = HEAD_DIM // 4
N_XATTN_HEADS = 4
XATTN_WIDTH = N_XATTN_HEADS * HEAD_DIM
NORM_EPS = 1e-6

LOG2E = math.log2(math.e)
ATTN_SCALE = HEAD_DIM ** -0.5
NEG_INF = float("-inf")

VMEM_LIMIT_BYTES = 52 * 1024 * 1024
LANES = 128
SUBLANES = 8
BF16_SUBLANES = 16
MXU_COLS = 256

NT_DIMS = (((1,), (1,)), ((), ()))


def _row_scale(xf):
    r = lax.rsqrt(jnp.mean(xf * xf, axis=-1, keepdims=True) + NORM_EPS)
    return jnp.broadcast_to(r, (xf.shape[0], LANES))


def _lanes(r, width):
    return jnp.concatenate([r] * (width // LANES), axis=1)


def _resident(shape):
    return pl.BlockSpec(shape, lambda *_: (0,) * len(shape), pipeline_mode=pl.Buffered(1))


def _cast_job(w, steps, gain=None, n_cols=None, col_block=0, step_of=lambda i, *_: i):
    n_rows = w.shape[0]
    n_cols = w.shape[1] if n_cols is None else n_cols
    rows = n_rows // steps
    assert rows * steps == n_rows and rows % BF16_SUBLANES == 0
    operands = [w]
    in_specs = [pl.BlockSpec((rows, n_cols), lambda *g: (step_of(*g), col_block))]
    if gain is not None:
        operands.append(gain.reshape(n_rows, 1).astype(F32))
        in_specs.append(pl.BlockSpec((rows, 1), lambda *g: (step_of(*g), 0)))
    out_spec = pl.BlockSpec((rows, n_cols), lambda *g: (step_of(*g), 0))
    return operands, in_specs, out_spec, jax.ShapeDtypeStruct((n_rows, n_cols), BF16)


def _call(body, name, grid, operands, in_specs, out_specs, out_shape, scratch_shapes=(), casts=(),
          prefetch=()):
    n_in = len(prefetch) + len(operands)
    n_out = len(out_shape)
    n_cast_in = [len(c[0]) for c in casts]

    def kern(*refs):
        ins, refs = refs[:n_in], refs[n_in:]
        cast_src, refs = refs[:sum(n_cast_in)], refs[sum(n_cast_in):]
        outs, refs = refs[:n_out], refs[n_out:]
        cast_dst, scratch = refs[:len(casts)], refs[len(casts):]
        if body is not None:
            body(*ins, *outs, *scratch)
        for n, dst in zip(n_cast_in, cast_dst):
            src, cast_src = cast_src[:n], cast_src[n:]
            w = src[0][...]
            dst[...] = (w if n == 1 else w * src[1][...]).astype(BF16)

    grid_spec = pltpu.PrefetchScalarGridSpec(
        num_scalar_prefetch=len(prefetch),
        grid=grid,
        in_specs=list(in_specs) + [s for c in casts for s in c[1]],
        out_specs=tuple(out_specs) + tuple(c[2] for c in casts),
        scratch_shapes=list(scratch_shapes),
    )
    outs = pl.pallas_call(
        kern,
        grid_spec=grid_spec,
        out_shape=tuple(out_shape) + tuple(c[3] for c in casts),
        compiler_params=pltpu.CompilerParams(
            dimension_semantics=("arbitrary",) * len(grid), vmem_limit_bytes=VMEM_LIMIT_BYTES),
        name=name,
    )(*prefetch, *operands, *[o for c in casts for o in c[0]])
    return outs[:n_out], outs[n_out:]


PROJ_TM = 512


def _rope(t, cos, sin_lo, sin_hi):
    return (t * cos + pltpu.roll(t, HEAD_DIM - ROPE_DIM // 2, 1) * sin_lo
            + pltpu.roll(t, ROPE_DIM // 2, 1) * sin_hi)


def _qkv_kernel(x_ref, w_ref, cos_ref, slo_ref, shi_ref, q_ref, k_ref, vt_ref, km_ref):
    tm = PROJ_TM
    xf = x_ref[...]
    x = xf.astype(BF16)
    r = _row_scale(xf)
    rq = r * (ATTN_SCALE * LOG2E)
    rv = _lanes(r, MXU_COLS)
    cos, slo, shi = cos_ref[...], slo_ref[...], shi_ref[...]
    heads_per_chunk = MXU_COLS // HEAD_DIM
    for c in range(ATTN_WIDTH // MXU_COLS):
        csl = slice(c * MXU_COLS, (c + 1) * MXU_COLS)
        qc = jnp.dot(x, w_ref[:, csl], preferred_element_type=F32)
        kc = jnp.dot(x, w_ref[:, ATTN_WIDTH + c * MXU_COLS:ATTN_WIDTH + (c + 1) * MXU_COLS],
                     preferred_element_type=F32)
        vc = jnp.dot(x, w_ref[:, 2 * ATTN_WIDTH + c * MXU_COLS:2 * ATTN_WIDTH + (c + 1) * MXU_COLS],
                     preferred_element_type=F32)
        vt_ref[0, csl, :] = (vc * rv).T.astype(BF16)
        for hh in range(heads_per_chunk):
            sl = slice(hh * HEAD_DIM, (hh + 1) * HEAD_DIM)
            osl = slice(c * MXU_COLS + hh * HEAD_DIM, c * MXU_COLS + (hh + 1) * HEAD_DIM)
            q_ref[:, osl] = (_rope(qc[:, sl], cos, slo, shi) * rq).astype(BF16)
            kf = _rope(kc[:, sl], cos, slo, shi) * r
            k_ref[:, osl] = kf.astype(BF16)
            for b in range(tm // MOBA_BLOCK):
                km_ref[b, :, osl] = jnp.mean(kf[b * MOBA_BLOCK:(b + 1) * MOBA_BLOCK], axis=0, keepdims=True)


def _qkv_proj(xf, w_qkv, cos, slo, shi, seq, casts):
    T, D = xf.shape
    tm = PROJ_TM
    tpb = seq // tm
    row = lambda i: (i, 0)
    pos = lambda i: (i % tpb, 0)
    return _call(
        _qkv_kernel, "qkv_proj", (T // tm,), (xf, w_qkv, cos, slo, shi),
        in_specs=[
            pl.BlockSpec((tm, D), row),
            _resident((D, 3 * ATTN_WIDTH)),
            pl.BlockSpec((tm, HEAD_DIM), pos),
            pl.BlockSpec((tm, HEAD_DIM), pos),
            pl.BlockSpec((tm, HEAD_DIM), pos),
        ],
        out_specs=[
            pl.BlockSpec((tm, ATTN_WIDTH), row),
            pl.BlockSpec((tm, ATTN_WIDTH), row),
            pl.BlockSpec((1, ATTN_WIDTH, tm), lambda i: (i, 0, 0)),
            pl.BlockSpec((tm // MOBA_BLOCK, 1, ATTN_WIDTH), lambda i: (i, 0, 0)),
        ],
        out_shape=[
            jax.ShapeDtypeStruct((T, ATTN_WIDTH), BF16),
            jax.ShapeDtypeStruct((T, ATTN_WIDTH), BF16),
            jax.ShapeDtypeStruct((T // tm, ATTN_WIDTH, tm), BF16),
            jax.ShapeDtypeStruct((T // MOBA_BLOCK, 1, ATTN_WIDTH), F32),
        ],
        casts=casts)


def _conv_kernel(tiles_per_batch, x_ref, w_ref, cw_ref, z_ref, u_scr):
    tm = PROJ_TM
    i = pl.program_id(0)

    @pl.when(i % tiles_per_batch == 0)
    def _():
        u_scr[0:SUBLANES, :] = jnp.zeros((SUBLANES, CONV_WIDTH), F32)

    xf = x_ref[...]
    x = xf.astype(BF16)
    r = _lanes(_row_scale(xf), MXU_COLS)
    rr = r * r
    for c in range(CONV_WIDTH // MXU_COLS):
        csl = slice(c * MXU_COLS, (c + 1) * MXU_COLS)
        cx = jnp.dot(x, w_ref[:, csl], preferred_element_type=F32)
        cb = jnp.dot(x, w_ref[:, CONV_WIDTH + c * MXU_COLS:CONV_WIDTH + (c + 1) * MXU_COLS],
                     preferred_element_type=F32)
        cc = jnp.dot(x, w_ref[:, 2 * CONV_WIDTH + c * MXU_COLS:2 * CONV_WIDTH + (c + 1) * MXU_COLS],
                     preferred_element_type=F32)
        u = cc * cx * rr
        u_scr[SUBLANES:tm + SUBLANES, csl] = u
        conv = (cw_ref[0:1, csl] * u_scr[SUBLANES - 2:tm + SUBLANES - 2, csl]
                + cw_ref[1:2, csl] * u_scr[SUBLANES - 1:tm + SUBLANES - 1, csl]
                + cw_ref[2:3, csl] * u)
        z_ref[:, csl] = (cb * r * conv).astype(BF16)
        u_scr[0:SUBLANES, csl] = u[tm - SUBLANES:tm]


def _conv_proj(xf, w_conv, conv_w, seq, casts):
    T, D = xf.shape
    tm = PROJ_TM
    return _call(
        functools.partial(_conv_kernel, seq // tm), "conv_proj", (T // tm,), (xf, w_conv, conv_w),
        in_specs=[
            pl.BlockSpec((tm, D), lambda i: (i, 0)),
            _resident((D, 3 * CONV_WIDTH)),
            pl.BlockSpec((CONV_K, CONV_WIDTH), lambda i: (0, 0)),
        ],
        out_specs=[pl.BlockSpec((tm, CONV_WIDTH), lambda i: (i, 0))],
        out_shape=[jax.ShapeDtypeStruct((T, CONV_WIDTH), BF16)],
        scratch_shapes=[pltpu.VMEM((tm + SUBLANES, CONV_WIDTH), F32)],
        casts=casts)


def _gate_kernel(x_ref, wa_ref, wc_ref, o_ref):
    xf = x_ref[...]
    x = xf.astype(BF16)
    r = _lanes(_row_scale(xf), MXU_COLS)
    for g, w_ref in enumerate((wa_ref, wc_ref)):
        n = w_ref.shape[1]
        for c in range(n // MXU_COLS):
            csl = slice(c * MXU_COLS, (c + 1) * MXU_COLS)
            o_ref[:, g * n + c * MXU_COLS:g * n + (c + 1) * MXU_COLS] = jax.nn.sigmoid(
                jnp.dot(x, w_ref[:, csl], preferred_element_type=F32) * r).astype(BF16)


def _gate_proj(xf, w_ga, w_gc, casts):
    T, D = xf.shape
    tm = PROJ_TM
    return _call(
        _gate_kernel, "gate_proj", (T // tm,), (xf, w_ga, w_gc),
        in_specs=[pl.BlockSpec((tm, D), lambda i: (i, 0)), _resident((D, D)), _resident((D, D))],
        out_specs=[pl.BlockSpec((tm, 2 * D), lambda i: (i, 0))],
        out_shape=[jax.ShapeDtypeStruct((T, 2 * D), BF16)],
        casts=casts)


MOBA_PAIR = 2
MOBA_UNROLL = 8
MASK_NEG = -1e30
ONES_ROWS = BF16_SUBLANES


def _moba_items(nb):
    qbs, prs = [], []
    for qb in range(nb):
        for pr in range((qb + MOBA_PAIR - 1) // MOBA_PAIR):
            qbs.append(qb)
            prs.append(pr)
    assert len(qbs) % MOBA_UNROLL == 0
    pad = 2 * MOBA_UNROLL
    tqb = np.asarray([nb] * pad + qbs + [nb] * pad, np.int32)
    tpr = np.asarray([0] * pad + prs + [0] * pad, np.int32)
    return tqb, tpr, len(qbs) // MOBA_UNROLL


def _moba_kernel(nb, n_trips, tqb_ref, tpr_ref, q_ref, k_ref, vt_ref, km_ref, hot_ref, o_ref,
                 qbias_scr, s_scr, p_scr, acc_scr, m_scr):
    blk = MOBA_BLOCK
    span = MOBA_PAIR * blk
    ones = jnp.ones((ONES_ROWS, span), BF16)

    km = km_ref[...]
    km_hi = km.astype(BF16)
    km_lo = (km - km_hi.astype(F32)).astype(BF16)
    bidx = lax.broadcasted_iota(jnp.int32, (nb, blk), 0)
    causal = (lax.broadcasted_iota(jnp.int32, (blk, blk), 0)
              <= lax.broadcasted_iota(jnp.int32, (blk, blk), 1))

    def choose_and_score(qb):
        rows = slice(qb * blk, (qb + 1) * blk)
        q = q_ref[rows, :]
        gate = (lax.dot_general(km_hi, q, NT_DIMS, preferred_element_type=F32)
                + lax.dot_general(km_lo, q, NT_DIMS, preferred_element_type=F32))
        g = jnp.where(bidx < qb, gate, NEG_INF)
        bias = jnp.full((nb, blk), MASK_NEG, F32)
        for _ in range(MOBA_TOPK):
            top = jnp.max(g, axis=0, keepdims=True)
            first = jnp.min(jnp.where(g == top, bidx, nb), axis=0, keepdims=True)
            hit = bidx == first
            bias = jnp.where(hit & (top > NEG_INF), 0.0, bias)
            g = jnp.where(hit, NEG_INF, g)
        bias = jnp.concatenate([bias, jnp.zeros((HEAD_DIM - nb, blk), F32)], axis=0)
        qbias_scr[rows, :] = bias.T.astype(BF16)
        return lax.dot_general(k_ref[rows, :], q, NT_DIMS, preferred_element_type=F32)

    def own_block(qb, s):
        s = jnp.where(causal, s, NEG_INF)
        m = jnp.max(s, axis=0, keepdims=True)
        p = jnp.exp2(s - m).astype(BF16)
        col = (qb % MOBA_PAIR) * blk
        vt = jnp.concatenate([vt_ref[qb // MOBA_PAIR, :, col:col + blk], ones[:, :blk]], axis=0)
        acc_scr[qb] = jnp.dot(vt, p, preferred_element_type=F32)
        m_scr[qb] = m

    own_lag = 2
    scored = [choose_and_score(qb) for qb in range(min(own_lag, nb))]
    for qb in range(nb):
        if qb + own_lag < nb:
            scored.append(choose_and_score(qb + own_lag))
        own_block(qb, scored[qb])

    acc_scr[nb] = jnp.zeros(acc_scr.shape[1:], F32)
    m_scr[nb] = jnp.zeros((1, blk), F32)
    s_scr[...] = jnp.zeros(s_scr.shape, F32)
    p_scr[...] = jnp.zeros(p_scr.shape, BF16)

    def scores(idx, slot):
        qb, pr = tqb_ref[idx], tpr_ref[idx]
        qrow = pl.multiple_of(jnp.minimum(qb, nb - 1) * blk, blk)
        krow = pl.multiple_of(pr * span, span)
        k_ext = jnp.concatenate([k_ref[pl.ds(krow, span), :], hot_ref[pl.ds(krow, span), :]], axis=1)
        q_ext = jnp.concatenate([q_ref[pl.ds(qrow, blk), :], qbias_scr[pl.ds(qrow, blk), :]], axis=1)
        s_scr[slot] = lax.dot_general(k_ext, q_ext, NT_DIMS, preferred_element_type=F32)

    def softmax(idx, slot, m):
        qb, pr = tqb_ref[idx], tpr_ref[idx]
        m = jnp.where(pr == 0, m_scr[qb], m)
        s = s_scr[slot]
        m_new = jnp.maximum(m, jnp.max(s, axis=0, keepdims=True))
        p_scr[slot] = jnp.exp2(s - m_new).astype(BF16)
        return m_new, jnp.exp2(m - m_new)

    def pv(idx, slot, alpha):
        qb, pr = tqb_ref[idx], tpr_ref[idx]
        vt = jnp.concatenate([vt_ref[pr], ones], axis=0)
        acc_scr[qb] = alpha * acc_scr[qb] + jnp.dot(vt, p_scr[slot], preferred_element_type=F32)

    def trip(t, carry):
        m, alphas = carry
        base = t * MOBA_UNROLL
        new_alphas = []
        for i in range(MOBA_UNROLL):
            pv(base + i, i, alphas[i])
            m, alpha = softmax(base + MOBA_UNROLL + i, i, m)
            new_alphas.append(alpha)
            scores(base + 2 * MOBA_UNROLL + i, i)
        return m, tuple(new_alphas)

    row0 = jnp.zeros((1, blk), F32)
    lax.fori_loop(0, n_trips + 2, trip, (row0, (row0,) * MOBA_UNROLL))

    for qb in range(nb):
        acc = acc_scr[qb]
        inv = 1.0 / acc[HEAD_DIM:HEAD_DIM + 1]
        o_ref[qb * blk:(qb + 1) * blk, :] = (acc[:HEAD_DIM] * inv).T.astype(BF16)


def _moba_attention(q, k, vt, kmean, batch, seq, casts):
    T = q.shape[0]
    blk = MOBA_BLOCK
    nb = seq // blk
    span = MOBA_PAIR * blk
    assert nb % MOBA_PAIR == 0 and nb <= HEAD_DIM and vt.shape == (T // span, ATTN_WIDTH, span)
    tqb, tpr, n_trips = _moba_items(nb)
    hot = jnp.asarray(np.arange(seq)[:, None] // blk == np.arange(HEAD_DIM)[None, :], BF16)
    head = lambda b, h, *_: (b, h)
    return _call(
        functools.partial(_moba_kernel, nb, n_trips), "moba_attention", (batch, N_ATTN_HEADS),
        (q, k, vt, kmean, hot),
        in_specs=[
            pl.BlockSpec((seq, HEAD_DIM), head),
            pl.BlockSpec((seq, HEAD_DIM), head),
            pl.BlockSpec((seq // span, HEAD_DIM, span), lambda b, h, *_: (b, h, 0)),
            pl.BlockSpec((nb, HEAD_DIM), head),
            _resident((seq, HEAD_DIM)),
        ],
        out_specs=[pl.BlockSpec((seq, HEAD_DIM), head)],
        out_shape=[jax.ShapeDtypeStruct((T, ATTN_WIDTH), BF16)],
        scratch_shapes=[
            pltpu.VMEM((seq, HEAD_DIM), BF16),
            pltpu.VMEM((MOBA_UNROLL, span, blk), F32),
            pltpu.VMEM((MOBA_UNROLL, span, blk), BF16),
            pltpu.VMEM((nb + 1, HEAD_DIM + ONES_ROWS, blk), F32),
            pltpu.VMEM((nb + 1, 1, blk), F32),
        ],
        casts=casts,
        prefetch=(jnp.asarray(tqb), jnp.asarray(tpr)))


MIX_TM = 256


def _mix_kernel(a_ref, z_ref, sga_ref, sgc_ref, x_ref, wa_ref, wc_ref, wm_ref, o_ref):
    y_attn = jnp.dot(a_ref[...], wa_ref[...], preferred_element_type=F32)
    y_conv = jnp.dot(z_ref[...], wc_ref[...], preferred_element_type=F32)
    merged = sga_ref[...].astype(F32) * y_attn + sgc_ref[...].astype(F32) * y_conv
    o_ref[...] = x_ref[...] + jnp.dot(merged.astype(BF16), wm_ref[...], preferred_element_type=F32)


def _mix(attn, z, gates, xf, w_attn_out, w_conv_out, w_mix_out, casts):
    T, D = xf.shape
    tm = MIX_TM
    row = lambda i: (i, 0)
    return _call(
        _mix_kernel, "mix", (T // tm,), (attn, z, gates, gates, xf, w_attn_out, w_conv_out, w_mix_out),
        in_specs=[
            pl.BlockSpec((tm, ATTN_WIDTH), row),
            pl.BlockSpec((tm, CONV_WIDTH), row),
            pl.BlockSpec((tm, D), row),
            pl.BlockSpec((tm, D), lambda i: (i, 1)),
            pl.BlockSpec((tm, D), row),
            _resident((ATTN_WIDTH, D)),
            _resident((CONV_WIDTH, D)),
            _resident((D, D)),
        ],
        out_specs=[pl.BlockSpec((tm, D), row)],
        out_shape=[jax.ShapeDtypeStruct((T, D), F32)],
        casts=casts)


def _mem_kv_kernel(mem_ref, w_ref, o_ref):
    mf = mem_ref[...]
    kv = jnp.dot(mf.astype(BF16), w_ref[...], preferred_element_type=F32)
    o_ref[...] = (kv * _lanes(_row_scale(mf), kv.shape[1])).astype(BF16)


def _mem_kv(memf, wkv):
    R, D = memf.shape
    N = wkv.shape[1]
    return _call(
        _mem_kv_kernel, "mem_kv", (1,), (memf, wkv),
        in_specs=[pl.BlockSpec((R, D), lambda i: (0, 0)), pl.BlockSpec((D, N), lambda i: (0, 0))],
        out_specs=[pl.BlockSpec((R, N), lambda i: (0, 0))],
        out_shape=[jax.ShapeDtypeStruct((R, N), BF16)])


XATTN_TM = 512


def _xattn_kernel(x_ref, wq_ref, kv_ref, wo_ref, o_ref):
    xf = x_ref[...]
    hq = jnp.dot(xf.astype(BF16), wq_ref[...], preferred_element_type=F32)
    hq = (hq * _lanes(_row_scale(xf) * (ATTN_SCALE * LOG2E), XATTN_WIDTH)).astype(BF16)
    outs = []
    for hh in range(N_XATTN_HEADS):
        sl = slice(hh * HEAD_DIM, (hh + 1) * HEAD_DIM)
        mk = kv_ref[:, sl]
        mv = kv_ref[:, XATTN_WIDTH + hh * HEAD_DIM:XATTN_WIDTH + (hh + 1) * HEAD_DIM]
        s = lax.dot_general(hq[:, sl], mk, NT_DIMS, preferred_element_type=F32)
        p = jnp.exp2(s - jnp.max(s, axis=-1, keepdims=True))
        inv = 1.0 / jnp.sum(p, axis=-1, keepdims=True)
        outs.append((jnp.dot(p.astype(BF16), mv, preferred_element_type=F32) * inv).astype(BF16))
    att = jnp.concatenate(outs, axis=-1)
    o_ref[...] = xf + jnp.dot(att, wo_ref[...], preferred_element_type=F32)


def _xattn(x1, wq, kv, wo, seq, n_mem):
    T, D = x1.shape
    tm = XATTN_TM
    tpb = seq // tm
    return _call(
        _xattn_kernel, "xattn", (T // tm,), (x1, wq, kv, wo),
        in_specs=[
            pl.BlockSpec((tm, D), lambda i: (i, 0)),
            _resident((D, XATTN_WIDTH)),
            pl.BlockSpec((n_mem, 2 * XATTN_WIDTH), lambda i: (i // tpb, 0)),
            _resident((XATTN_WIDTH, D)),
        ],
        out_specs=[pl.BlockSpec((tm, D), lambda i: (i, 0))],
        out_shape=[jax.ShapeDtypeStruct((T, D), F32)])


MLP_TM = 512
MLP_TF = 1024


def _mlp_kernel(x_ref, wu_ref, wd_ref, gf_ref, o_ref, h_scr, acc_scr):
    f = pl.program_id(1)

    @pl.when(f == 0)
    def _():
        xf = x_ref[...]
        h_scr[...] = xf.astype(BF16)
        acc_scr[...] = xf

    up = jnp.dot(h_scr[...], wu_ref[...], preferred_element_type=F32)
    u = jnp.maximum(up * _lanes(_row_scale(x_ref[...]), up.shape[1]), 0.0)
    acc_scr[...] += jnp.dot((u * u).astype(BF16), wd_ref[...], preferred_element_type=F32)

    @pl.when(f == pl.num_programs(1) - 1)
    def _():
        y = acc_scr[...]
        ms = jnp.mean(y * y, axis=-1, keepdims=True)
        o_ref[...] = y * lax.rsqrt(ms + NORM_EPS) * gf_ref[...]


def _mlp(x2, w_up, w_down, g_final):
    T, D = x2.shape
    FF = w_up.shape[1]
    tm, tf = MLP_TM, MLP_TF
    return _call(
        _mlp_kernel, "mlp", (T // tm, FF // tf), (x2, w_up, w_down, g_final),
        in_specs=[
            pl.BlockSpec((tm, D), lambda i, f: (i, 0)),
            pl.BlockSpec((D, tf), lambda i, f: (0, f)),
            pl.BlockSpec((tf, D), lambda i, f: (f, 0)),
            pl.BlockSpec((1, D), lambda i, f: (0, 0)),
        ],
        out_specs=[pl.BlockSpec((tm, D), lambda i, f: (i, 0))],
        out_shape=[jax.ShapeDtypeStruct((T, D), F32)],
        scratch_shapes=[pltpu.VMEM((tm, D), BF16), pltpu.VMEM((tm, D), F32)])


def _rope_tables(seq):
    half = ROPE_DIM // 2
    inv_freq = ROPE_THETA ** (-jnp.arange(half, dtype=F32) / half)
    ang = jnp.arange(seq).astype(F32)[:, None] * inv_freq[None, :]
    cos, sin = jnp.cos(ang), jnp.sin(ang)
    pad = HEAD_DIM - ROPE_DIM
    cos_t = jnp.concatenate([cos, cos, jnp.ones((seq, pad), F32)], axis=-1)
    sin_lo = jnp.concatenate([-sin, jnp.zeros((seq, half + pad), F32)], axis=-1)
    sin_hi = jnp.concatenate([jnp.zeros((seq, half), F32), sin, jnp.zeros((seq, pad), F32)], axis=-1)
    return cos_t, sin_lo, sin_hi


PREP_STEPS = 16


def _layer(xf, memf, batch, seq, n_mem, norm_mix, w_in, conv_w, w_attn_out, w_conv_out, w_mix_out,
           norm_xattn, norm_mem, wq_x, wkv_x, wo_x, norm_mlp, w_up, w_down, g_out):
    T, D = xf.shape
    cos_t, sin_lo, sin_hi = _rope_tables(seq)
    n_qkv, n_conv = 3 * ATTN_WIDTH, 3 * CONV_WIDTH
    assert n_qkv == n_conv and (n_qkv + n_conv) % D == 0
    gate_block = (n_qkv + n_conv) // D
    proj_steps = T // PROJ_TM
    head_step = lambda b, h, *_: b * N_ATTN_HEADS + h

    _, (w_qkv,) = _call(None, "prep_weights", (PREP_STEPS,), (), [], [], [],
                        casts=[_cast_job(w_in, PREP_STEPS, norm_mix, n_qkv, 0)])
    (q, k, vt, kmean), (w_conv, w_ga, w_gc) = _qkv_proj(
        xf, w_qkv, cos_t, sin_lo, sin_hi, seq,
        [_cast_job(w_in, proj_steps, norm_mix, n_conv, 1),
         _cast_job(w_in, proj_steps, norm_mix, D, gate_block),
         _cast_job(w_in, proj_steps, norm_mix, D, gate_block + 1)])
    (z,), (wa, wc, wm) = _conv_proj(
        xf, w_conv, conv_w, seq,
        [_cast_job(w_attn_out, proj_steps), _cast_job(w_conv_out, proj_steps), _cast_job(w_mix_out, proj_steps)])
    (gates,), (wq, wkv, wo) = _gate_proj(
        xf, w_ga, w_gc,
        [_cast_job(wq_x, proj_steps, norm_xattn), _cast_job(wkv_x, proj_steps, norm_mem),
         _cast_job(wo_x, proj_steps)])
    kmean = kmean.reshape(T // MOBA_BLOCK, ATTN_WIDTH)
    (attn,), (wu,) = _moba_attention(
        q, k, vt, kmean, batch, seq,
        [_cast_job(w_up, batch * N_ATTN_HEADS, norm_mlp, step_of=head_step)])
    (x1,), (wd,) = _mix(attn, z, gates, xf, wa, wc, wm, [_cast_job(w_down, T // MIX_TM)])
    (kv,), _ = _mem_kv(memf, wkv)
    (x2,), _ = _xattn(x1, wq, kv, wo, seq, n_mem)
    (out,), _ = _mlp(x2, wu, wd, g_out)
    return out


def kernel(x, mem, norm_mix, w_in, conv_w, w_attn_out, w_conv_out, w_mix_out, norm_xattn, norm_mem,
           wq_x, wkv_x, wo_x, norm_mlp, w_up, w_down, norm_final):
    B, S, D = x.shape
    n_mem = mem.shape[1]
    depth = w_in.shape[0]
    assert depth == 1, "the final rmsnorm is fused into the last layer's MLP kernel"
    xf = x.reshape(B * S, D)
    memf = mem.reshape(B * n_mem, D)
    out = _layer(xf, memf, B, S, n_mem, norm_mix[0], w_in[0], conv_w[0], w_attn_out[0], w_conv_out[0],
                 w_mix_out[0], norm_xattn[0], norm_mem[0], wq_x[0], wkv_x[0], wo_x[0], norm_mlp[0],
                 w_up[0], w_down[0], norm_final.reshape(1, D).astype(F32))
    return out.reshape(B, S, D)
```

```python
import functools
import math

import jax
import jax.numpy as jnp
import numpy as np
from jax import lax
from jax.experimental import pallas as pl
from jax.experimental.pallas import tpu as pltpu

F32 = jnp.float32
BF16 = jnp.bfloat16

HEAD_DIM = 128
N_ATTN_HEADS = 8
ATTN_WIDTH = N_ATTN_HEADS * HEAD_DIM
CONV_WIDTH = 1024
CONV_K = 3
MOBA_BLOCK = 256
MOBA_TOPK = 3
ROPE_THETA = 500000.0
ROPE_DIM = HEAD_DIM // 4
N_XATTN_HEADS = 4
XATTN_WIDTH = N_XATTN_HEADS * HEAD_DIM
NORM_EPS = 1e-6

LOG2E = math.log2(math.e)
ATTN_SCALE = HEAD_DIM ** -0.5
NEG_INF = float("-inf")

VMEM_LIMIT_BYTES = 52 * 1024 * 1024
SUBLANES = 8
BF16_SUBLANES = 16
MXU_COLS = 256

NT_DIMS = (((1,), (1,)), ((), ()))


def _rms_norm_bf16(xf, g):
    ms = jnp.mean(xf * xf, axis=-1, keepdims=True)
    return (xf * lax.rsqrt(ms + NORM_EPS) * g).astype(BF16)


def _resident(shape):
    return pl.BlockSpec(shape, lambda *_: (0,) * len(shape), pipeline_mode=pl.Buffered(1))


def _cast_job(w, steps, n_cols=None, col_block=0, step_of=lambda i, *_: i):
    n_rows = w.shape[0]
    n_cols = w.shape[1] if n_cols is None else n_cols
    rows = n_rows // steps
    assert rows * steps == n_rows and rows % BF16_SUBLANES == 0
    in_spec = pl.BlockSpec((rows, n_cols), lambda *g: (step_of(*g), col_block))
    out_spec = pl.BlockSpec((rows, n_cols), lambda *g: (step_of(*g), 0))
    return w, in_spec, out_spec, jax.ShapeDtypeStruct((n_rows, n_cols), BF16)


def _call(body, name, grid, operands, in_specs, out_specs, out_shape, scratch_shapes=(), casts=(),
          prefetch=()):
    n_in = len(prefetch) + len(operands)
    n_out = len(out_shape)
    n_cast = len(casts)

    def kern(*refs):
        ins, refs = refs[:n_in], refs[n_in:]
        cast_src, refs = refs[:n_cast], refs[n_cast:]
        outs, refs = refs[:n_out], refs[n_out:]
        cast_dst, scratch = refs[:n_cast], refs[n_cast:]
        body(*ins, *outs, *scratch)
        for src, dst in zip(cast_src, cast_dst):
            dst[...] = src[...].astype(BF16)

    grid_spec = pltpu.PrefetchScalarGridSpec(
        num_scalar_prefetch=len(prefetch),
        grid=grid,
        in_specs=list(in_specs) + [c[1] for c in casts],
        out_specs=tuple(out_specs) + tuple(c[2] for c in casts),
        scratch_shapes=list(scratch_shapes),
    )
    outs = pl.pallas_call(
        kern,
        grid_spec=grid_spec,
        out_shape=tuple(out_shape) + tuple(c[3] for c in casts),
        compiler_params=pltpu.CompilerParams(
            dimension_semantics=("arbitrary",) * len(grid), vmem_limit_bytes=VMEM_LIMIT_BYTES),
        name=name,
    )(*prefetch, *operands, *[c[0] for c in casts])
    return outs[:n_out], outs[n_out:]


NORM_TM = 1024


def _norm_kernel(x_ref, g_ref, o_ref):
    o_ref[...] = _rms_norm_bf16(x_ref[...], g_ref[...])


def _norm(xf, g, casts):
    T, D = xf.shape
    tm = NORM_TM
    return _call(
        _norm_kernel, "norm_mix", (T // tm,), (xf, g),
        in_specs=[pl.BlockSpec((tm, D), lambda i: (i, 0)), pl.BlockSpec((1, D), lambda i: (0, 0))],
        out_specs=[pl.BlockSpec((tm, D), lambda i: (i, 0))],
        out_shape=[jax.ShapeDtypeStruct((T, D), BF16)],
        casts=casts)


PROJ_TM = 512


def _rope(t, cos, sin_lo, sin_hi):
    return (t * cos + pltpu.roll(t, HEAD_DIM - ROPE_DIM // 2, 1) * sin_lo
            + pltpu.roll(t, ROPE_DIM // 2, 1) * sin_hi)


def _qkv_kernel(h_ref, w_ref, cos_ref, slo_ref, shi_ref, q_ref, k_ref, vt_ref, km_ref):
    tm = PROJ_TM
    h = h_ref[...]
    cos, slo, shi = cos_ref[...], slo_ref[...], shi_ref[...]
    heads_per_chunk = MXU_COLS // HEAD_DIM
    for c in range(ATTN_WIDTH // MXU_COLS):
        csl = slice(c * MXU_COLS, (c + 1) * MXU_COLS)
        qc = jnp.dot(h, w_ref[:, csl], preferred_element_type=F32)
        kc = jnp.dot(h, w_ref[:, ATTN_WIDTH + c * MXU_COLS:ATTN_WIDTH + (c + 1) * MXU_COLS],
                     preferred_element_type=F32)
        vc = jnp.dot(h, w_ref[:, 2 * ATTN_WIDTH + c * MXU_COLS:2 * ATTN_WIDTH + (c + 1) * MXU_COLS],
                     preferred_element_type=F32)
        vt_ref[0, csl, :] = vc.T.astype(BF16)
        for hh in range(heads_per_chunk):
            sl = slice(hh * HEAD_DIM, (hh + 1) * HEAD_DIM)
            osl = slice(c * MXU_COLS + hh * HEAD_DIM, c * MXU_COLS + (hh + 1) * HEAD_DIM)
            q_ref[:, osl] = (_rope(qc[:, sl], cos, slo, shi) * (ATTN_SCALE * LOG2E)).astype(BF16)
            kf = _rope(kc[:, sl], cos, slo, shi)
            k_ref[:, osl] = kf.astype(BF16)
            for r in range(tm // MOBA_BLOCK):
                km_ref[r, :, osl] = jnp.mean(kf[r * MOBA_BLOCK:(r + 1) * MOBA_BLOCK], axis=0, keepdims=True)


def _qkv_proj(h, w_qkv, cos, slo, shi, seq, casts):
    T, D = h.shape
    tm = PROJ_TM
    tpb = seq // tm
    row = lambda i: (i, 0)
    pos = lambda i: (i % tpb, 0)
    return _call(
        _qkv_kernel, "qkv_proj", (T // tm,), (h, w_qkv, cos, slo, shi),
        in_specs=[
            pl.BlockSpec((tm, D), row),
            _resident((D, 3 * ATTN_WIDTH)),
            pl.BlockSpec((tm, HEAD_DIM), pos),
            pl.BlockSpec((tm, HEAD_DIM), pos),
            pl.BlockSpec((tm, HEAD_DIM), pos),
        ],
        out_specs=[
            pl.BlockSpec((tm, ATTN_WIDTH), row),
            pl.BlockSpec((tm, ATTN_WIDTH), row),
            pl.BlockSpec((1, ATTN_WIDTH, tm), lambda i: (i, 0, 0)),
            pl.BlockSpec((tm // MOBA_BLOCK, 1, ATTN_WIDTH), lambda i: (i, 0, 0)),
        ],
        out_shape=[
            jax.ShapeDtypeStruct((T, ATTN_WIDTH), BF16),
            jax.ShapeDtypeStruct((T, ATTN_WIDTH), BF16),
            jax.ShapeDtypeStruct((T // tm, ATTN_WIDTH, tm), BF16),
            jax.ShapeDtypeStruct((T // MOBA_BLOCK, 1, ATTN_WIDTH), F32),
        ],
        casts=casts)


def _conv_kernel(tiles_per_batch, h_ref, w_ref, cw_ref, z_ref, u_scr):
    tm = PROJ_TM
    i = pl.program_id(0)

    @pl.when(i % tiles_per_batch == 0)
    def _():
        u_scr[0:SUBLANES, :] = jnp.zeros((SUBLANES, CONV_WIDTH), F32)

    h = h_ref[...]
    for c in range(CONV_WIDTH // MXU_COLS):
        csl = slice(c * MXU_COLS, (c + 1) * MXU_COLS)
        cx = jnp.dot(h, w_ref[:, csl], preferred_element_type=F32)
        cb = jnp.dot(h, w_ref[:, CONV_WIDTH + c * MXU_COLS:CONV_WIDTH + (c + 1) * MXU_COLS],
                     preferred_element_type=F32)
        cc = jnp.dot(h, w_ref[:, 2 * CONV_WIDTH + c * MXU_COLS:2 * CONV_WIDTH + (c + 1) * MXU_COLS],
                     preferred_element_type=F32)
        u = cc * cx
        u_scr[SUBLANES:tm + SUBLANES, csl] = u
        conv = (cw_ref[0:1, csl] * u_scr[SUBLANES - 2:tm + SUBLANES - 2, csl]
                + cw_ref[1:2, csl] * u_scr[SUBLANES - 1:tm + SUBLANES - 1, csl]
                + cw_ref[2:3, csl] * u)
        z_ref[:, csl] = (cb * conv).astype(BF16)
        u_scr[0:SUBLANES, csl] = u[tm - SUBLANES:tm]


def _conv_proj(h, w_conv, conv_w, seq, casts):
    T, D = h.shape
    tm = PROJ_TM
    return _call(
        functools.partial(_conv_kernel, seq // tm), "conv_proj", (T // tm,), (h, w_conv, conv_w),
        in_specs=[
            pl.BlockSpec((tm, D), lambda i: (i, 0)),
            _resident((D, 3 * CONV_WIDTH)),
            pl.BlockSpec((CONV_K, CONV_WIDTH), lambda i: (0, 0)),
        ],
        out_specs=[pl.BlockSpec((tm, CONV_WIDTH), lambda i: (i, 0))],
        out_shape=[jax.ShapeDtypeStruct((T, CONV_WIDTH), BF16)],
        scratch_shapes=[pltpu.VMEM((tm + SUBLANES, CONV_WIDTH), F32)],
        casts=casts)


def _gate_kernel(h_ref, wa_ref, wc_ref, o_ref):
    h = h_ref[...]
    for g, w_ref in enumerate((wa_ref, wc_ref)):
        n = w_ref.shape[1]
        for c in range(n // MXU_COLS):
            csl = slice(c * MXU_COLS, (c + 1) * MXU_COLS)
            o_ref[:, g * n + c * MXU_COLS:g * n + (c + 1) * MXU_COLS] = jax.nn.sigmoid(
                jnp.dot(h, w_ref[:, csl], preferred_element_type=F32)).astype(BF16)


def _gate_proj(h, w_ga, w_gc, casts):
    T, D = h.shape
    tm = PROJ_TM
    return _call(
        _gate_kernel, "gate_proj", (T // tm,), (h, w_ga, w_gc),
        in_specs=[pl.BlockSpec((tm, D), lambda i: (i, 0)), _resident((D, D)), _resident((D, D))],
        out_specs=[pl.BlockSpec((tm, 2 * D), lambda i: (i, 0))],
        out_shape=[jax.ShapeDtypeStruct((T, 2 * D), BF16)],
        casts=casts)


MOBA_PAIR = 2
MOBA_UNROLL = 16
MASK_NEG = -1e30
ONES_ROWS = BF16_SUBLANES


def _moba_items(nb):
    qbs, prs = [], []
    for qb in range(nb):
        for pr in range((qb + MOBA_PAIR - 1) // MOBA_PAIR):
            qbs.append(qb)
            prs.append(pr)
    assert len(qbs) % MOBA_UNROLL == 0
    return np.asarray(qbs, np.int32), np.asarray(prs, np.int32), len(qbs) // MOBA_UNROLL


def _moba_kernel(nb, n_trips, tqb_ref, tpr_ref, q_ref, k_ref, vt_ref, km_ref, o_ref,
                 qbias_scr, s_scr, smax_scr, acc_scr, m_scr):
    blk = MOBA_BLOCK
    span = MOBA_PAIR * blk
    vt_blocks = vt_ref.shape[2] // blk
    vt_tiles = MOBA_PAIR // vt_blocks
    ones = jnp.ones((ONES_ROWS, span), BF16)
    lane_row = lax.broadcasted_iota(jnp.int32, (1, HEAD_DIM), 1)

    km = km_ref[...]
    km_hi = km.astype(BF16)
    km_lo = (km - km_hi.astype(F32)).astype(BF16)
    bidx = lax.broadcasted_iota(jnp.int32, (nb, blk), 0)
    causal = (lax.broadcasted_iota(jnp.int32, (blk, blk), 0)
              <= lax.broadcasted_iota(jnp.int32, (blk, blk), 1))

    def choose_and_score(qb):
        rows = slice(qb * blk, (qb + 1) * blk)
        q = q_ref[rows, :]
        gate = (lax.dot_general(km_hi, q, NT_DIMS, preferred_element_type=F32)
                + lax.dot_general(km_lo, q, NT_DIMS, preferred_element_type=F32))
        g = jnp.where(bidx < qb, gate, NEG_INF)
        bias = jnp.full((nb, blk), MASK_NEG, F32)
        for _ in range(MOBA_TOPK):
            top = jnp.max(g, axis=0, keepdims=True)
            first = jnp.min(jnp.where(g == top, bidx, nb), axis=0, keepdims=True)
            hit = bidx == first
            bias = jnp.where(hit & (top > NEG_INF), 0.0, bias)
            g = jnp.where(hit, NEG_INF, g)
        bias = jnp.concatenate([bias, jnp.zeros((HEAD_DIM - nb, blk), F32)], axis=0)
        qbias_scr[rows, :] = bias.T.astype(BF16)
        return lax.dot_general(k_ref[rows, :], q, NT_DIMS, preferred_element_type=F32)

    def own_block(qb, s):
        s = jnp.where(causal, s, NEG_INF)
        m = jnp.max(s, axis=0, keepdims=True)
        p = jnp.exp2(s - m).astype(BF16)
        col = (qb % vt_blocks) * blk
        vt = jnp.concatenate([vt_ref[qb // vt_blocks, :, col:col + blk], ones[:, :blk]], axis=0)
        acc_scr[qb] = jnp.dot(vt, p, preferred_element_type=F32)
        m_scr[qb] = m

    own_lag = 2
    scored = [choose_and_score(qb) for qb in range(min(own_lag, nb))]
    for qb in range(nb):
        if qb + own_lag < nb:
            scored.append(choose_and_score(qb + own_lag))
        own_block(qb, scored[qb])

    def scores(idx, slot):
        qb, pr = tqb_ref[idx], tpr_ref[idx]
        qrow = pl.multiple_of(qb * blk, blk)
        krow = pl.multiple_of(pr * span, span)
        hot = jnp.concatenate(
            [jnp.broadcast_to(jnp.where(lane_row == pr * MOBA_PAIR + u, 1.0, 0.0).astype(BF16), (blk, HEAD_DIM))
             for u in range(MOBA_PAIR)], axis=0)
        k_ext = jnp.concatenate([k_ref[pl.ds(krow, span), :], hot], axis=1)
        q_ext = jnp.concatenate([q_ref[pl.ds(qrow, blk), :], qbias_scr[pl.ds(qrow, blk), :]], axis=1)
        s = lax.dot_general(k_ext, q_ext, NT_DIMS, preferred_element_type=F32)
        s_scr[slot] = s
        smax_scr[slot] = jnp.max(s.reshape(span // SUBLANES, SUBLANES, blk), axis=0)

    def softmax_pv(idx, slot, m):
        qb, pr = tqb_ref[idx], tpr_ref[idx]
        m = jnp.where(pr == 0, m_scr[qb], m)
        m_new = jnp.maximum(m, jnp.max(smax_scr[slot], axis=0, keepdims=True))
        p = jnp.exp2(s_scr[slot] - m_new).astype(BF16)
        vt = jnp.concatenate([vt_ref[pr * vt_tiles + j] for j in range(vt_tiles)], axis=1)
        vt = jnp.concatenate([vt, ones], axis=0)
        acc_scr[qb] = jnp.exp2(m - m_new) * acc_scr[qb] + jnp.dot(vt, p, preferred_element_type=F32)
        return m_new

    def trip(t, m):
        base = t * MOBA_UNROLL
        for i in range(MOBA_UNROLL):
            m = softmax_pv(base + i, i, m)
            scores(base + MOBA_UNROLL + i, i)
        return m

    for i in range(MOBA_UNROLL):
        scores(i, i)
    m = lax.fori_loop(0, n_trips - 1, trip, jnp.zeros((1, blk), F32))
    for i in range(MOBA_UNROLL):
        m = softmax_pv((n_trips - 1) * MOBA_UNROLL + i, i, m)

    for qb in range(nb):
        acc = acc_scr[qb]
        inv = 1.0 / acc[HEAD_DIM:HEAD_DIM + 1]
        o_ref[qb * blk:(qb + 1) * blk, :] = (acc[:HEAD_DIM] * inv).T.astype(BF16)


def _moba_attention(q, k, vt, kmean, batch, seq, casts):
    T = q.shape[0]
    blk = MOBA_BLOCK
    nb = seq // blk
    span = MOBA_PAIR * blk
    vt_span = vt.shape[2]
    assert nb % MOBA_PAIR == 0 and nb <= HEAD_DIM and span % vt_span == 0
    assert vt.shape == (T // vt_span, ATTN_WIDTH, vt_span)
    tqb, tpr, n_trips = _moba_items(nb)
    head = lambda b, h, *_: (b, h)
    return _call(
        functools.partial(_moba_kernel, nb, n_trips), "moba_attention", (batch, N_ATTN_HEADS),
        (q, k, vt, kmean),
        in_specs=[
            pl.BlockSpec((seq, HEAD_DIM), head),
            pl.BlockSpec((seq, HEAD_DIM), head),
            pl.BlockSpec((seq // vt_span, HEAD_DIM, vt_span), lambda b, h, *_: (b, h, 0)),
            pl.BlockSpec((nb, HEAD_DIM), head),
        ],
        out_specs=[pl.BlockSpec((seq, HEAD_DIM), head)],
        out_shape=[jax.ShapeDtypeStruct((T, ATTN_WIDTH), BF16)],
        scratch_shapes=[
            pltpu.VMEM((seq, HEAD_DIM), BF16),
            pltpu.VMEM((MOBA_UNROLL, span, blk), F32),
            pltpu.VMEM((MOBA_UNROLL, SUBLANES, blk), F32),
            pltpu.VMEM((nb, HEAD_DIM + ONES_ROWS, blk), F32),
            pltpu.VMEM((nb, 1, blk), F32),
        ],
        casts=casts,
        prefetch=(jnp.asarray(tqb), jnp.asarray(tpr)))


MIX_TM = 256


def _mix_kernel(a_ref, z_ref, sga_ref, sgc_ref, x_ref, wa_ref, wc_ref, wm_ref, o_ref):
    y_attn = jnp.dot(a_ref[...], wa_ref[...], preferred_element_type=F32)
    y_conv = jnp.dot(z_ref[...], wc_ref[...], preferred_element_type=F32)
    merged = sga_ref[...].astype(F32) * y_attn + sgc_ref[...].astype(F32) * y_conv
    o_ref[...] = x_ref[...] + jnp.dot(merged.astype(BF16), wm_ref[...], preferred_element_type=F32)


def _mix(attn, z, gates, xf, w_attn_out, w_conv_out, w_mix_out, casts):
    T, D = xf.shape
    tm = MIX_TM
    row = lambda i: (i, 0)
    return _call(
        _mix_kernel, "mix", (T // tm,), (attn, z, gates, gates, xf, w_attn_out, w_conv_out, w_mix_out),
        in_specs=[
            pl.BlockSpec((tm, ATTN_WIDTH), row),
            pl.BlockSpec((tm, CONV_WIDTH), row),
            pl.BlockSpec((tm, D), row),
            pl.BlockSpec((tm, D), lambda i: (i, 1)),
            pl.BlockSpec((tm, D), row),
            _resident((ATTN_WIDTH, D)),
            _resident((CONV_WIDTH, D)),
            _resident((D, D)),
        ],
        out_specs=[pl.BlockSpec((tm, D), row)],
        out_shape=[jax.ShapeDtypeStruct((T, D), F32)],
        casts=casts)


def _mem_kv_kernel(mem_ref, g_ref, w_ref, o_ref):
    o_ref[...] = jnp.dot(_rms_norm_bf16(mem_ref[...], g_ref[...]), w_ref[...],
                         preferred_element_type=F32).astype(BF16)


def _mem_kv(memf, g, wkv):
    R, D = memf.shape
    N = wkv.shape[1]
    return _call(
        _mem_kv_kernel, "mem_kv", (1,), (memf, g, wkv),
        in_specs=[pl.BlockSpec((R, D), lambda i: (0, 0)),
                  pl.BlockSpec((1, D), lambda i: (0, 0)),
                  pl.BlockSpec((D, N), lambda i: (0, 0))],
        out_specs=[pl.BlockSpec((R, N), lambda i: (0, 0))],
        out_shape=[jax.ShapeDtypeStruct((R, N), BF16)])


XATTN_TM = 512


def _xattn_kernel(x_ref, g_ref, wq_ref, kv_ref, wo_ref, o_ref):
    xf = x_ref[...]
    hq = jnp.dot(_rms_norm_bf16(xf, g_ref[...]), wq_ref[...], preferred_element_type=F32)
    hq = (hq * (ATTN_SCALE * LOG2E)).astype(BF16)
    outs = []
    for hh in range(N_XATTN_HEADS):
        sl = slice(hh * HEAD_DIM, (hh + 1) * HEAD_DIM)
        mk = kv_ref[:, sl]
        mv = kv_ref[:, XATTN_WIDTH + hh * HEAD_DIM:XATTN_WIDTH + (hh + 1) * HEAD_DIM]
        s = lax.dot_general(hq[:, sl], mk, NT_DIMS, preferred_element_type=F32)
        p = jnp.exp2(s - jnp.max(s, axis=-1, keepdims=True))
        inv = 1.0 / jnp.sum(p, axis=-1, keepdims=True)
        outs.append((jnp.dot(p.astype(BF16), mv, preferred_element_type=F32) * inv).astype(BF16))
    att = jnp.concatenate(outs, axis=-1)
    o_ref[...] = xf + jnp.dot(att, wo_ref[...], preferred_element_type=F32)


def _xattn(x1, g, wq, kv, wo, seq, n_mem):
    T, D = x1.shape
    tm = XATTN_TM
    tpb = seq // tm
    return _call(
        _xattn_kernel, "xattn", (T // tm,), (x1, g, wq, kv, wo),
        in_specs=[
            pl.BlockSpec((tm, D), lambda i: (i, 0)),
            pl.BlockSpec((1, D), lambda i: (0, 0)),
            _resident((D, XATTN_WIDTH)),
            pl.BlockSpec((n_mem, 2 * XATTN_WIDTH), lambda i: (i // tpb, 0)),
            _resident((XATTN_WIDTH, D)),
        ],
        out_specs=[pl.BlockSpec((tm, D), lambda i: (i, 0))],
        out_shape=[jax.ShapeDtypeStruct((T, D), F32)])


MLP_TM = 512
MLP_TF = 1024


def _mlp_kernel(x_ref, g_ref, wu_ref, wd_ref, gf_ref, o_ref, h_scr, acc_scr):
    f = pl.program_id(1)

    @pl.when(f == 0)
    def _():
        xf = x_ref[...]
        h_scr[...] = _rms_norm_bf16(xf, g_ref[...])
        acc_scr[...] = xf

    u = jnp.maximum(jnp.dot(h_scr[...], wu_ref[...], preferred_element_type=F32), 0.0)
    acc_scr[...] += jnp.dot((u * u).astype(BF16), wd_ref[...], preferred_element_type=F32)

    @pl.when(f == pl.num_programs(1) - 1)
    def _():
        y = acc_scr[...]
        ms = jnp.mean(y * y, axis=-1, keepdims=True)
        o_ref[...] = y * lax.rsqrt(ms + NORM_EPS) * gf_ref[...]


def _mlp(x2, g, w_up, w_down, g_final):
    T, D = x2.shape
    FF = w_up.shape[1]
    tm, tf = MLP_TM, MLP_TF
    return _call(
        _mlp_kernel, "mlp", (T // tm, FF // tf), (x2, g, w_up, w_down, g_final),
        in_specs=[
            pl.BlockSpec((tm, D), lambda i, f: (i, 0)),
            pl.BlockSpec((1, D), lambda i, f: (0, 0)),
            pl.BlockSpec((D, tf), lambda i, f: (0, f)),
            pl.BlockSpec((tf, D), lambda i, f: (f, 0)),
            pl.BlockSpec((1, D), lambda i, f: (0, 0)),
        ],
        out_specs=[pl.BlockSpec((tm, D), lambda i, f: (i, 0))],
        out_shape=[jax.ShapeDtypeStruct((T, D), F32)],
        scratch_shapes=[pltpu.VMEM((tm, D), BF16), pltpu.VMEM((tm, D), F32)])


def _rope_tables(seq):
    half = ROPE_DIM // 2
    inv_freq = ROPE_THETA ** (-jnp.arange(half, dtype=F32) / half)
    ang = jnp.arange(seq).astype(F32)[:, None] * inv_freq[None, :]
    cos, sin = jnp.cos(ang), jnp.sin(ang)
    pad = HEAD_DIM - ROPE_DIM
    cos_t = jnp.concatenate([cos, cos, jnp.ones((seq, pad), F32)], axis=-1)
    sin_lo = jnp.concatenate([-sin, jnp.zeros((seq, half + pad), F32)], axis=-1)
    sin_hi = jnp.concatenate([jnp.zeros((seq, half), F32), sin, jnp.zeros((seq, pad), F32)], axis=-1)
    return cos_t, sin_lo, sin_hi


def _layer(xf, memf, batch, seq, n_mem, norm_mix, w_in, conv_w, w_attn_out, w_conv_out, w_mix_out,
           norm_xattn, norm_mem, wq_x, wkv_x, wo_x, norm_mlp, w_up, w_down, g_out):
    T, D = xf.shape
    row = lambda g: g.reshape(1, D).astype(F32)
    cos_t, sin_lo, sin_hi = _rope_tables(seq)
    n_qkv, n_conv = 3 * ATTN_WIDTH, 3 * CONV_WIDTH
    assert n_qkv == n_conv and (n_qkv + n_conv) % D == 0
    proj_steps = T // PROJ_TM
    head_step = lambda b, h, *_: b * N_ATTN_HEADS + h

    (h,), (w_qkv,) = _norm(xf, row(norm_mix), [_cast_job(w_in, T // NORM_TM, n_qkv, 0)])
    (q, k, vt, kmean), (w_conv, w_ga, w_gc) = _qkv_proj(
        h, w_qkv, cos_t, sin_lo, sin_hi, seq,
        [_cast_job(w_in, proj_steps, n_conv, 1),
         _cast_job(w_in, proj_steps, D, (n_qkv + n_conv) // D),
         _cast_job(w_in, proj_steps, D, (n_qkv + n_conv) // D + 1)])
    (z,), (wa, wc, wm) = _conv_proj(
        h, w_conv, conv_w, seq,
        [_cast_job(w_attn_out, proj_steps), _cast_job(w_conv_out, proj_steps), _cast_job(w_mix_out, proj_steps)])
    (gates,), (wq, wkv, wo) = _gate_proj(
        h, w_ga, w_gc, [_cast_job(wq_x, proj_steps), _cast_job(wkv_x, proj_steps), _cast_job(wo_x, proj_steps)])
    kmean = kmean.reshape(T // MOBA_BLOCK, ATTN_WIDTH)
    (attn,), (wu,) = _moba_attention(
        q, k, vt, kmean, batch, seq, [_cast_job(w_up, batch * N_ATTN_HEADS, step_of=head_step)])
    (x1,), (wd,) = _mix(attn, z, gates, xf, wa, wc, wm, [_cast_job(w_down, T // MIX_TM)])
    (kv,), _ = _mem_kv(memf, row(norm_mem), wkv)
    (x2,), _ = _xattn(x1, row(norm_xattn), wq, kv, wo, seq, n_mem)
    (out,), _ = _mlp(x2, row(norm_mlp), wu, wd, g_out)
    return out


def kernel(x, mem, norm_mix, w_in, conv_w, w_attn_out, w_conv_out, w_mix_out, norm_xattn, norm_mem,
           wq_x, wkv_x, wo_x, norm_mlp, w_up, w_down, norm_final):
    B, S, D = x.shape
    n_mem = mem.shape[1]
    depth = w_in.shape[0]
    assert depth == 1, "the final rmsnorm is fused into the last layer's MLP kernel"
    xf = x.reshape(B * S, D)
    memf = mem.reshape(B * n_mem, D)
    out = _layer(xf, memf, B, S, n_mem, norm_mix[0], w_in[0], conv_w[0], w_attn_out[0], w_conv_out[0],
                 w_mix_out[0], norm_xattn[0], norm_mem[0], wq_x[0], wkv_x[0], wo_x[0], norm_mlp[0],
                 w_up[0], w_down[0], norm_final.reshape(1, D).astype(F32))
    return out.reshape(B, S, D)
```

```python
import functools
import math

import jax
import jax.numpy as jnp
import numpy as np
from jax import lax
from jax.experimental import pallas as pl
from jax.experimental.pallas import tpu as pltpu

F32 = jnp.float32
BF16 = jnp.bfloat16

HEAD_DIM = 128
N_ATTN_HEADS = 8
ATTN_WIDTH = N_ATTN_HEADS * HEAD_DIM
CONV_WIDTH = 1024
CONV_K = 3
MOBA_BLOCK = 256
MOBA_TOPK = 3
ROPE_THETA = 500000.0
ROPE_DIM = HEAD_DIM // 4
N_XATTN_HEADS = 4
XATTN_WIDTH = N_XATTN_HEADS * HEAD_DIM
NORM_EPS = 1e-6

LOG2E = math.log2(math.e)
ATTN_SCALE = HEAD_DIM ** -0.5
NEG_INF = float("-inf")

VMEM_LIMIT_BYTES = 52 * 1024 * 1024
SUBLANES = 8
BF16_SUBLANES = 16
MXU_COLS = 256

NT_DIMS = (((1,), (1,)), ((), ()))


def _rms_norm_bf16(xf, g):
    ms = jnp.mean(xf * xf, axis=-1, keepdims=True)
    return (xf * lax.rsqrt(ms + NORM_EPS) * g).astype(BF16)


def _resident(shape):
    return pl.BlockSpec(shape, lambda *_: (0,) * len(shape), pipeline_mode=pl.Buffered(1))


def _cast_job(w, steps, n_cols=None, col_block=0, step_of=lambda i, *_: i):
    n_rows = w.shape[0]
    n_cols = w.shape[1] if n_cols is None else n_cols
    rows = n_rows // steps
    assert rows * steps == n_rows and rows % BF16_SUBLANES == 0
    in_spec = pl.BlockSpec((rows, n_cols), lambda *g: (step_of(*g), col_block))
    out_spec = pl.BlockSpec((rows, n_cols), lambda *g: (step_of(*g), 0))
    return w, in_spec, out_spec, jax.ShapeDtypeStruct((n_rows, n_cols), BF16)


def _call(body, name, grid, operands, in_specs, out_specs, out_shape, scratch_shapes=(), casts=(),
          prefetch=()):
    n_in = len(prefetch) + len(operands)
    n_out = len(out_shape)
    n_cast = len(casts)

    def kern(*refs):
        ins, refs = refs[:n_in], refs[n_in:]
        cast_src, refs = refs[:n_cast], refs[n_cast:]
        outs, refs = refs[:n_out], refs[n_out:]
        cast_dst, scratch = refs[:n_cast], refs[n_cast:]
        body(*ins, *outs, *scratch)
        for src, dst in zip(cast_src, cast_dst):
            dst[...] = src[...].astype(BF16)

    grid_spec = pltpu.PrefetchScalarGridSpec(
        num_scalar_prefetch=len(prefetch),
        grid=grid,
        in_specs=list(in_specs) + [c[1] for c in casts],
        out_specs=tuple(out_specs) + tuple(c[2] for c in casts),
        scratch_shapes=list(scratch_shapes),
    )
    outs = pl.pallas_call(
        kern,
        grid_spec=grid_spec,
        out_shape=tuple(out_shape) + tuple(c[3] for c in casts),
        compiler_params=pltpu.CompilerParams(
            dimension_semantics=("arbitrary",) * len(grid), vmem_limit_bytes=VMEM_LIMIT_BYTES),
        name=name,
    )(*prefetch, *operands, *[c[0] for c in casts])
    return outs[:n_out], outs[n_out:]


NORM_TM = 1024


def _norm_kernel(x_ref, g_ref, o_ref):
    o_ref[...] = _rms_norm_bf16(x_ref[...], g_ref[...])


def _norm(xf, g, casts):
    T, D = xf.shape
    tm = NORM_TM
    return _call(
        _norm_kernel, "norm_mix", (T // tm,), (xf, g),
        in_specs=[pl.BlockSpec((tm, D), lambda i: (i, 0)), pl.BlockSpec((1, D), lambda i: (0, 0))],
        out_specs=[pl.BlockSpec((tm, D), lambda i: (i, 0))],
        out_shape=[jax.ShapeDtypeStruct((T, D), BF16)],
        casts=casts)


PROJ_TM = 1024
VT_SPAN = 512


def _rope(t, cos, sin_lo, sin_hi):
    return (t * cos + pltpu.roll(t, HEAD_DIM - ROPE_DIM // 2, 1) * sin_lo
            + pltpu.roll(t, ROPE_DIM // 2, 1) * sin_hi)


def _qkv_kernel(h_ref, w_ref, cos_ref, slo_ref, shi_ref, q_ref, k_ref, vt_ref, km_ref):
    tm = PROJ_TM
    h = h_ref[...]
    cos, slo, shi = cos_ref[...], slo_ref[...], shi_ref[...]
    heads_per_chunk = MXU_COLS // HEAD_DIM
    for c in range(ATTN_WIDTH // MXU_COLS):
        csl = slice(c * MXU_COLS, (c + 1) * MXU_COLS)
        qc = jnp.dot(h, w_ref[:, csl], preferred_element_type=F32)
        kc = jnp.dot(h, w_ref[:, ATTN_WIDTH + c * MXU_COLS:ATTN_WIDTH + (c + 1) * MXU_COLS],
                     preferred_element_type=F32)
        vc = jnp.dot(h, w_ref[:, 2 * ATTN_WIDTH + c * MXU_COLS:2 * ATTN_WIDTH + (c + 1) * MXU_COLS],
                     preferred_element_type=F32)
        for r in range(tm // VT_SPAN):
            vt_ref[r, csl, :] = vc[r * VT_SPAN:(r + 1) * VT_SPAN].T.astype(BF16)
        for hh in range(heads_per_chunk):
            sl = slice(hh * HEAD_DIM, (hh + 1) * HEAD_DIM)
            osl = slice(c * MXU_COLS + hh * HEAD_DIM, c * MXU_COLS + (hh + 1) * HEAD_DIM)
            q_ref[:, osl] = (_rope(qc[:, sl], cos, slo, shi) * (ATTN_SCALE * LOG2E)).astype(BF16)
            kf = _rope(kc[:, sl], cos, slo, shi)
            k_ref[:, osl] = kf.astype(BF16)
            for r in range(tm // MOBA_BLOCK):
                km_ref[r, :, osl] = jnp.mean(kf[r * MOBA_BLOCK:(r + 1) * MOBA_BLOCK], axis=0, keepdims=True)


def _qkv_proj(h, w_qkv, cos, slo, shi, seq, casts):
    T, D = h.shape
    tm = PROJ_TM
    tpb = seq // tm
    row = lambda i: (i, 0)
    pos = lambda i: (i % tpb, 0)
    return _call(
        _qkv_kernel, "qkv_proj", (T // tm,), (h, w_qkv, cos, slo, shi),
        in_specs=[
            pl.BlockSpec((tm, D), row),
            _resident((D, 3 * ATTN_WIDTH)),
            pl.BlockSpec((tm, HEAD_DIM), pos),
            pl.BlockSpec((tm, HEAD_DIM), pos),
            pl.BlockSpec((tm, HEAD_DIM), pos),
        ],
        out_specs=[
            pl.BlockSpec((tm, ATTN_WIDTH), row),
            pl.BlockSpec((tm, ATTN_WIDTH), row),
            pl.BlockSpec((tm // VT_SPAN, ATTN_WIDTH, VT_SPAN), lambda i: (i, 0, 0)),
            pl.BlockSpec((tm // MOBA_BLOCK, 1, ATTN_WIDTH), lambda i: (i, 0, 0)),
        ],
        out_shape=[
            jax.ShapeDtypeStruct((T, ATTN_WIDTH), BF16),
            jax.ShapeDtypeStruct((T, ATTN_WIDTH), BF16),
            jax.ShapeDtypeStruct((T // VT_SPAN, ATTN_WIDTH, VT_SPAN), BF16),
            jax.ShapeDtypeStruct((T // MOBA_BLOCK, 1, ATTN_WIDTH), F32),
        ],
        casts=casts)


def _conv_kernel(tiles_per_batch, h_ref, w_ref, cw_ref, z_ref, u_scr):
    tm = PROJ_TM
    i = pl.program_id(0)

    @pl.when(i % tiles_per_batch == 0)
    def _():
        u_scr[0:SUBLANES, :] = jnp.zeros((SUBLANES, CONV_WIDTH), F32)

    h = h_ref[...]
    for c in range(CONV_WIDTH // MXU_COLS):
        csl = slice(c * MXU_COLS, (c + 1) * MXU_COLS)
        cx = jnp.dot(h, w_ref[:, csl], preferred_element_type=F32)
        cb = jnp.dot(h, w_ref[:, CONV_WIDTH + c * MXU_COLS:CONV_WIDTH + (c + 1) * MXU_COLS],
                     preferred_element_type=F32)
        cc = jnp.dot(h, w_ref[:, 2 * CONV_WIDTH + c * MXU_COLS:2 * CONV_WIDTH + (c + 1) * MXU_COLS],
                     preferred_element_type=F32)
        u = cc * cx
        u_scr[SUBLANES:tm + SUBLANES, csl] = u
        conv = (cw_ref[0:1, csl] * u_scr[SUBLANES - 2:tm + SUBLANES - 2, csl]
                + cw_ref[1:2, csl] * u_scr[SUBLANES - 1:tm + SUBLANES - 1, csl]
                + cw_ref[2:3, csl] * u)
        z_ref[:, csl] = (cb * conv).astype(BF16)
        u_scr[0:SUBLANES, csl] = u[tm - SUBLANES:tm]


def _conv_proj(h, w_conv, conv_w, seq, casts):
    T, D = h.shape
    tm = PROJ_TM
    return _call(
        functools.partial(_conv_kernel, seq // tm), "conv_proj", (T // tm,), (h, w_conv, conv_w),
        in_specs=[
            pl.BlockSpec((tm, D), lambda i: (i, 0)),
            _resident((D, 3 * CONV_WIDTH)),
            pl.BlockSpec((CONV_K, CONV_WIDTH), lambda i: (0, 0)),
        ],
        out_specs=[pl.BlockSpec((tm, CONV_WIDTH), lambda i: (i, 0))],
        out_shape=[jax.ShapeDtypeStruct((T, CONV_WIDTH), BF16)],
        scratch_shapes=[pltpu.VMEM((tm + SUBLANES, CONV_WIDTH), F32)],
        casts=casts)


def _gate_kernel(h_ref, wa_ref, wc_ref, o_ref):
    h = h_ref[...]
    for g, w_ref in enumerate((wa_ref, wc_ref)):
        n = w_ref.shape[1]
        for c in range(n // MXU_COLS):
            csl = slice(c * MXU_COLS, (c + 1) * MXU_COLS)
            o_ref[:, g * n + c * MXU_COLS:g * n + (c + 1) * MXU_COLS] = jax.nn.sigmoid(
                jnp.dot(h, w_ref[:, csl], preferred_element_type=F32)).astype(BF16)


def _gate_proj(h, w_ga, w_gc, casts):
    T, D = h.shape
    tm = PROJ_TM
    return _call(
        _gate_kernel, "gate_proj", (T // tm,), (h, w_ga, w_gc),
        in_specs=[pl.BlockSpec((tm, D), lambda i: (i, 0)), _resident((D, D)), _resident((D, D))],
        out_specs=[pl.BlockSpec((tm, 2 * D), lambda i: (i, 0))],
        out_shape=[jax.ShapeDtypeStruct((T, 2 * D), BF16)],
        casts=casts)


MOBA_PAIR = 2
MOBA_UNROLL = 16
MASK_NEG = -1e30
ONES_ROWS = BF16_SUBLANES


def _moba_items(nb):
    qbs, prs = [], []
    for qb in range(nb):
        for pr in range((qb + MOBA_PAIR - 1) // MOBA_PAIR):
            qbs.append(qb)
            prs.append(pr)
    assert len(qbs) % MOBA_UNROLL == 0
    return np.asarray(qbs, np.int32), np.asarray(prs, np.int32), len(qbs) // MOBA_UNROLL


def _moba_kernel(nb, n_trips, tqb_ref, tpr_ref, q_ref, k_ref, vt_ref, km_ref, o_ref,
                 qbias_scr, s_scr, smax_scr, acc_scr, m_scr):
    blk = MOBA_BLOCK
    span = MOBA_PAIR * blk
    vt_blocks = vt_ref.shape[2] // blk
    vt_tiles = MOBA_PAIR // vt_blocks
    ones = jnp.ones((ONES_ROWS, span), BF16)
    lane_row = lax.broadcasted_iota(jnp.int32, (1, HEAD_DIM), 1)

    km = km_ref[...]
    km_hi = km.astype(BF16)
    km_lo = (km - km_hi.astype(F32)).astype(BF16)
    bidx = lax.broadcasted_iota(jnp.int32, (nb, blk), 0)
    causal = (lax.broadcasted_iota(jnp.int32, (blk, blk), 0)
              <= lax.broadcasted_iota(jnp.int32, (blk, blk), 1))

    def choose_and_score(qb):
        rows = slice(qb * blk, (qb + 1) * blk)
        q = q_ref[rows, :]
        gate = (lax.dot_general(km_hi, q, NT_DIMS, preferred_element_type=F32)
                + lax.dot_general(km_lo, q, NT_DIMS, preferred_element_type=F32))
        g = jnp.where(bidx < qb, gate, NEG_INF)
        bias = jnp.full((nb, blk), MASK_NEG, F32)
        for _ in range(MOBA_TOPK):
            top = jnp.max(g, axis=0, keepdims=True)
            first = jnp.min(jnp.where(g == top, bidx, nb), axis=0, keepdims=True)
            hit = bidx == first
            bias = jnp.where(hit & (top > NEG_INF), 0.0, bias)
            g = jnp.where(hit, NEG_INF, g)
        bias = jnp.concatenate([bias, jnp.zeros((HEAD_DIM - nb, blk), F32)], axis=0)
        qbias_scr[rows, :] = bias.T.astype(BF16)
        return lax.dot_general(k_ref[rows, :], q, NT_DIMS, preferred_element_type=F32)

    def own_block(qb, s):
        s = jnp.where(causal, s, NEG_INF)
        m = jnp.max(s, axis=0, keepdims=True)
        p = jnp.exp2(s - m).astype(BF16)
        col = (qb % vt_blocks) * blk
        vt = jnp.concatenate([vt_ref[qb // vt_blocks, :, col:col + blk], ones[:, :blk]], axis=0)
        acc_scr[qb] = jnp.dot(vt, p, preferred_element_type=F32)
        m_scr[qb] = m

    def scores(idx, slot):
        qb, pr = tqb_ref[idx], tpr_ref[idx]
        qrow = pl.multiple_of(qb * blk, blk)
        krow = pl.multiple_of(pr * span, span)
        hot = jnp.concatenate(
            [jnp.broadcast_to(jnp.where(lane_row == pr * MOBA_PAIR + u, 1.0, 0.0).astype(BF16), (blk, HEAD_DIM))
             for u in range(MOBA_PAIR)], axis=0)
        k_ext = jnp.concatenate([k_ref[pl.ds(krow, span), :], hot], axis=1)
        q_ext = jnp.concatenate([q_ref[pl.ds(qrow, blk), :], qbias_scr[pl.ds(qrow, blk), :]], axis=1)
        s = lax.dot_general(k_ext, q_ext, NT_DIMS, preferred_element_type=F32)
        s_scr[slot] = s
        smax_scr[slot] = jnp.max(s.reshape(span // SUBLANES, SUBLANES, blk), axis=0)

    def softmax_pv(idx, slot, m):
        qb, pr = tqb_ref[idx], tpr_ref[idx]
        m = jnp.where(pr == 0, m_scr[qb], m)
        m_new = jnp.maximum(m, jnp.max(smax_scr[slot], axis=0, keepdims=True))
        p = jnp.exp2(s_scr[slot] - m_new).astype(BF16)
        vt = jnp.concatenate([vt_ref[pr * vt_tiles + j] for j in range(vt_tiles)], axis=1)
        vt = jnp.concatenate([vt, ones], axis=0)
        acc_scr[qb] = jnp.exp2(m - m_new) * acc_scr[qb] + jnp.dot(vt, p, preferred_element_type=F32)
        return m_new

    def trip(t, m):
        base = t * MOBA_UNROLL
        for i in range(MOBA_UNROLL):
            m = softmax_pv(base + i, i, m)
            scores(base + MOBA_UNROLL + i, i)
        return m

    own_lag = 2
    scored = [choose_and_score(qb) for qb in range(min(own_lag, nb))]
    for qb in range(nb):
        if qb + own_lag < nb:
            scored.append(choose_and_score(qb + own_lag))
        own_block(qb, scored[qb])
    for i in range(MOBA_UNROLL):
        scores(i, i)

    m = lax.fori_loop(0, n_trips - 1, trip, jnp.zeros((1, blk), F32))
    for i in range(MOBA_UNROLL):
        m = softmax_pv((n_trips - 1) * MOBA_UNROLL + i, i, m)

    for qb in range(nb):
        acc = acc_scr[qb]
        inv = 1.0 / acc[HEAD_DIM:HEAD_DIM + 1]
        o_ref[qb * blk:(qb + 1) * blk, :] = (acc[:HEAD_DIM] * inv).T.astype(BF16)


def _moba_attention(q, k, vt, kmean, batch, seq, casts):
    T = q.shape[0]
    blk = MOBA_BLOCK
    nb = seq // blk
    span = MOBA_PAIR * blk
    vt_span = vt.shape[2]
    assert nb % MOBA_PAIR == 0 and nb <= HEAD_DIM and span % vt_span == 0
    assert vt.shape == (T // vt_span, ATTN_WIDTH, vt_span)
    tqb, tpr, n_trips = _moba_items(nb)
    head = lambda b, h, *_: (b, h)
    return _call(
        functools.partial(_moba_kernel, nb, n_trips), "moba_attention", (batch, N_ATTN_HEADS),
        (q, k, vt, kmean),
        in_specs=[
            pl.BlockSpec((seq, HEAD_DIM), head),
            pl.BlockSpec((seq, HEAD_DIM), head),
            pl.BlockSpec((seq // vt_span, HEAD_DIM, vt_span), lambda b, h, *_: (b, h, 0)),
            pl.BlockSpec((nb, HEAD_DIM), head),
        ],
        out_specs=[pl.BlockSpec((seq, HEAD_DIM), head)],
        out_shape=[jax.ShapeDtypeStruct((T, ATTN_WIDTH), BF16)],
        scratch_shapes=[
            pltpu.VMEM((seq, HEAD_DIM), BF16),
            pltpu.VMEM((MOBA_UNROLL, span, blk), F32),
            pltpu.VMEM((MOBA_UNROLL, SUBLANES, blk), F32),
            pltpu.VMEM((nb, HEAD_DIM + ONES_ROWS, blk), F32),
            pltpu.VMEM((nb, 1, blk), F32),
        ],
        casts=casts,
        prefetch=(jnp.asarray(tqb), jnp.asarray(tpr)))


MIX_TM = 256


def _mix_kernel(a_ref, z_ref, sga_ref, sgc_ref, x_ref, wa_ref, wc_ref, wm_ref, o_ref):
    y_attn = jnp.dot(a_ref[...], wa_ref[...], preferred_element_type=F32)
    y_conv = jnp.dot(z_ref[...], wc_ref[...], preferred_element_type=F32)
    merged = sga_ref[...].astype(F32) * y_attn + sgc_ref[...].astype(F32) * y_conv
    o_ref[...] = x_ref[...] + jnp.dot(merged.astype(BF16), wm_ref[...], preferred_element_type=F32)


def _mix(attn, z, gates, xf, w_attn_out, w_conv_out, w_mix_out, casts):
    T, D = xf.shape
    tm = MIX_TM
    row = lambda i: (i, 0)
    return _call(
        _mix_kernel, "mix", (T // tm,), (attn, z, gates, gates, xf, w_attn_out, w_conv_out, w_mix_out),
        in_specs=[
            pl.BlockSpec((tm, ATTN_WIDTH), row),
            pl.BlockSpec((tm, CONV_WIDTH), row),
            pl.BlockSpec((tm, D), row),
            pl.BlockSpec((tm, D), lambda i: (i, 1)),
            pl.BlockSpec((tm, D), row),
            _resident((ATTN_WIDTH, D)),
            _resident((CONV_WIDTH, D)),
            _resident((D, D)),
        ],
        out_specs=[pl.BlockSpec((tm, D), row)],
        out_shape=[jax.ShapeDtypeStruct((T, D), F32)],
        casts=casts)


def _mem_kv_kernel(mem_ref, g_ref, w_ref, o_ref):
    o_ref[...] = jnp.dot(_rms_norm_bf16(mem_ref[...], g_ref[...]), w_ref[...],
                         preferred_element_type=F32).astype(BF16)


def _mem_kv(memf, g, wkv):
    R, D = memf.shape
    N = wkv.shape[1]
    return _call(
        _mem_kv_kernel, "mem_kv", (1,), (memf, g, wkv),
        in_specs=[pl.BlockSpec((R, D), lambda i: (0, 0)),
                  pl.BlockSpec((1, D), lambda i: (0, 0)),
                  pl.BlockSpec((D, N), lambda i: (0, 0))],
        out_specs=[pl.BlockSpec((R, N), lambda i: (0, 0))],
        out_shape=[jax.ShapeDtypeStruct((R, N), BF16)])


XATTN_TM = 1024


def _xattn_kernel(x_ref, g_ref, wq_ref, kv_ref, wo_ref, o_ref):
    xf = x_ref[...]
    hq = jnp.dot(_rms_norm_bf16(xf, g_ref[...]), wq_ref[...], preferred_element_type=F32)
    hq = (hq * (ATTN_SCALE * LOG2E)).astype(BF16)
    outs = []
    for hh in range(N_XATTN_HEADS):
        sl = slice(hh * HEAD_DIM, (hh + 1) * HEAD_DIM)
        mk = kv_ref[:, sl]
        mv = kv_ref[:, XATTN_WIDTH + hh * HEAD_DIM:XATTN_WIDTH + (hh + 1) * HEAD_DIM]
        s = lax.dot_general(hq[:, sl], mk, NT_DIMS, preferred_element_type=F32)
        p = jnp.exp2(s - jnp.max(s, axis=-1, keepdims=True))
        inv = 1.0 / jnp.sum(p, axis=-1, keepdims=True)
        outs.append((jnp.dot(p.astype(BF16), mv, preferred_element_type=F32) * inv).astype(BF16))
    att = jnp.concatenate(outs, axis=-1)
    o_ref[...] = xf + jnp.dot(att, wo_ref[...], preferred_element_type=F32)


def _xattn(x1, g, wq, kv, wo, seq, n_mem):
    T, D = x1.shape
    tm = XATTN_TM
    tpb = seq // tm
    return _call(
        _xattn_kernel, "xattn", (T // tm,), (x1, g, wq, kv, wo),
        in_specs=[
            pl.BlockSpec((tm, D), lambda i: (i, 0)),
            pl.BlockSpec((1, D), lambda i: (0, 0)),
            _resident((D, XATTN_WIDTH)),
            pl.BlockSpec((n_mem, 2 * XATTN_WIDTH), lambda i: (i // tpb, 0)),
            _resident((XATTN_WIDTH, D)),
        ],
        out_specs=[pl.BlockSpec((tm, D), lambda i: (i, 0))],
        out_shape=[jax.ShapeDtypeStruct((T, D), F32)])


MLP_TM = 512
MLP_TF = 1024


def _mlp_kernel(x_ref, g_ref, wu_ref, wd_ref, gf_ref, o_ref, h_scr, acc_scr):
    f = pl.program_id(1)

    @pl.when(f == 0)
    def _():
        xf = x_ref[...]
        h_scr[...] = _rms_norm_bf16(xf, g_ref[...])
        acc_scr[...] = xf

    u = jnp.maximum(jnp.dot(h_scr[...], wu_ref[...], preferred_element_type=F32), 0.0)
    acc_scr[...] += jnp.dot((u * u).astype(BF16), wd_ref[...], preferred_element_type=F32)

    @pl.when(f == pl.num_programs(1) - 1)
    def _():
        y = acc_scr[...]
        ms = jnp.mean(y * y, axis=-1, keepdims=True)
        o_ref[...] = y * lax.rsqrt(ms + NORM_EPS) * gf_ref[...]


def _mlp(x2, g, w_up, w_down, g_final):
    T, D = x2.shape
    FF = w_up.shape[1]
    tm, tf = MLP_TM, MLP_TF
    return _call(
        _mlp_kernel, "mlp", (T // tm, FF // tf), (x2, g, w_up, w_down, g_final),
        in_specs=[
            pl.BlockSpec((tm, D), lambda i, f: (i, 0)),
            pl.BlockSpec((1, D), lambda i, f: (0, 0)),
            pl.BlockSpec((D, tf), lambda i, f: (0, f)),
            pl.BlockSpec((tf, D), lambda i, f: (f, 0)),
            pl.BlockSpec((1, D), lambda i, f: (0, 0)),
        ],
        out_specs=[pl.BlockSpec((tm, D), lambda i, f: (i, 0))],
        out_shape=[jax.ShapeDtypeStruct((T, D), F32)],
        scratch_shapes=[pltpu.VMEM((tm, D), BF16), pltpu.VMEM((tm, D), F32)])


def _rope_tables(seq):
    half = ROPE_DIM // 2
    inv_freq = ROPE_THETA ** (-jnp.arange(half, dtype=F32) / half)
    ang = jnp.arange(seq).astype(F32)[:, None] * inv_freq[None, :]
    cos, sin = jnp.cos(ang), jnp.sin(ang)
    pad = HEAD_DIM - ROPE_DIM
    cos_t = jnp.concatenate([cos, cos, jnp.ones((seq, pad), F32)], axis=-1)
    sin_lo = jnp.concatenate([-sin, jnp.zeros((seq, half + pad), F32)], axis=-1)
    sin_hi = jnp.concatenate([jnp.zeros((seq, half), F32), sin, jnp.zeros((seq, pad), F32)], axis=-1)
    return cos_t, sin_lo, sin_hi


def _layer(xf, memf, batch, seq, n_mem, norm_mix, w_in, conv_w, w_attn_out, w_conv_out, w_mix_out,
           norm_xattn, norm_mem, wq_x, wkv_x, wo_x, norm_mlp, w_up, w_down, g_out):
    T, D = xf.shape
    row = lambda g: g.reshape(1, D).astype(F32)
    cos_t, sin_lo, sin_hi = _rope_tables(seq)
    n_qkv, n_conv = 3 * ATTN_WIDTH, 3 * CONV_WIDTH
    assert n_qkv == n_conv and (n_qkv + n_conv) % D == 0
    proj_steps = T // PROJ_TM
    head_step = lambda b, h, *_: b * N_ATTN_HEADS + h

    (h,), (w_qkv,) = _norm(xf, row(norm_mix), [_cast_job(w_in, T // NORM_TM, n_qkv, 0)])
    (q, k, vt, kmean), (w_conv, w_ga, w_gc) = _qkv_proj(
        h, w_qkv, cos_t, sin_lo, sin_hi, seq,
        [_cast_job(w_in, proj_steps, n_conv, 1),
         _cast_job(w_in, proj_steps, D, (n_qkv + n_conv) // D),
         _cast_job(w_in, proj_steps, D, (n_qkv + n_conv) // D + 1)])
    (z,), (wa, wc, wm) = _conv_proj(
        h, w_conv, conv_w, seq,
        [_cast_job(w_attn_out, proj_steps), _cast_job(w_conv_out, proj_steps), _cast_job(w_mix_out, proj_steps)])
    (gates,), (wq, wkv, wo) = _gate_proj(
        h, w_ga, w_gc, [_cast_job(wq_x, proj_steps), _cast_job(wkv_x, proj_steps), _cast_job(wo_x, proj_steps)])
    kmean = kmean.reshape(T // MOBA_BLOCK, ATTN_WIDTH)
    (attn,), (wu,) = _moba_attention(
        q, k, vt, kmean, batch, seq, [_cast_job(w_up, batch * N_ATTN_HEADS, step_of=head_step)])
    (x1,), (wd,) = _mix(attn, z, gates, xf, wa, wc, wm, [_cast_job(w_down, T // MIX_TM)])
    (kv,), _ = _mem_kv(memf, row(norm_mem), wkv)
    (x2,), _ = _xattn(x1, row(norm_xattn), wq, kv, wo, seq, n_mem)
    (out,), _ = _mlp(x2, row(norm_mlp), wu, wd, g_out)
    return out


def kernel(x, mem, norm_mix, w_in, conv_w, w_attn_out, w_conv_out, w_mix_out, norm_xattn, norm_mem,
           wq_x, wkv_x, wo_x, norm_mlp, w_up, w_down, norm_final):
    B, S, D = x.shape
    n_mem = mem.shape[1]
    depth = w_in.shape[0]
    assert depth == 1, "the final rmsnorm is fused into the last layer's MLP kernel"
    xf = x.reshape(B * S, D)
    memf = mem.reshape(B * n_mem, D)
    out = _layer(xf, memf, B, S, n_mem, norm_mix[0], w_in[0], conv_w[0], w_attn_out[0], w_conv_out[0],
                 w_mix_out[0], norm_xattn[0], norm_mem[0], wq_x[0], wkv_x[0], wo_x[0], norm_mlp[0],
                 w_up[0], w_down[0], norm_final.reshape(1, D).astype(F32))
    return out.reshape(B, S, D)
```

```python
import functools
import math

import jax
import jax.numpy as jnp
import numpy as np
from jax import lax
from jax.experimental import pallas as pl
from jax.experimental.pallas import tpu as pltpu

F32 = jnp.float32
BF16 = jnp.bfloat16

HEAD_DIM = 128
N_ATTN_HEADS = 8
ATTN_WIDTH = N_ATTN_HEADS * HEAD_DIM
CONV_WIDTH = 1024
CONV_K = 3
MOBA_BLOCK = 256
MOBA_TOPK = 3
ROPE_THETA = 500000.0
ROPE_DIM = HEAD_DIM // 4
N_XATTN_HEADS = 4
XATTN_WIDTH = N_XATTN_HEADS * HEAD_DIM
NORM_EPS = 1e-6

LOG2E = math.log2(math.e)
ATTN_SCALE = HEAD_DIM ** -0.5
NEG_INF = float("-inf")

VMEM_LIMIT_BYTES = 58 * 1024 * 1024
SUBLANES = 8
BF16_SUBLANES = 16
MXU_COLS = 256

NT_DIMS = (((1,), (1,)), ((), ()))


def _rms_norm_bf16(xf, g):
    ms = jnp.mean(xf * xf, axis=-1, keepdims=True)
    return (xf * lax.rsqrt(ms + NORM_EPS) * g).astype(BF16)


def _resident(shape):
    return pl.BlockSpec(shape, lambda *_: (0,) * len(shape), pipeline_mode=pl.Buffered(1))


def _cast_job(w, steps, n_cols=None, col_block=0, step_of=lambda i, *_: i):
    n_rows = w.shape[0]
    n_cols = w.shape[1] if n_cols is None else n_cols
    rows = n_rows // steps
    assert rows * steps == n_rows and rows % BF16_SUBLANES == 0
    in_spec = pl.BlockSpec((rows, n_cols), lambda *g: (step_of(*g), col_block))
    out_spec = pl.BlockSpec((rows, n_cols), lambda *g: (step_of(*g), 0))
    return w, in_spec, out_spec, jax.ShapeDtypeStruct((n_rows, n_cols), BF16)


def _call(body, name, grid, operands, in_specs, out_specs, out_shape, scratch_shapes=(), casts=(),
          prefetch=()):
    n_in = len(prefetch) + len(operands)
    n_out = len(out_shape)
    n_cast = len(casts)

    def kern(*refs):
        ins, refs = refs[:n_in], refs[n_in:]
        cast_src, refs = refs[:n_cast], refs[n_cast:]
        outs, refs = refs[:n_out], refs[n_out:]
        cast_dst, scratch = refs[:n_cast], refs[n_cast:]
        body(*ins, *outs, *scratch)
        for src, dst in zip(cast_src, cast_dst):
            dst[...] = src[...].astype(BF16)

    grid_spec = pltpu.PrefetchScalarGridSpec(
        num_scalar_prefetch=len(prefetch),
        grid=grid,
        in_specs=list(in_specs) + [c[1] for c in casts],
        out_specs=tuple(out_specs) + tuple(c[2] for c in casts),
        scratch_shapes=list(scratch_shapes),
    )
    outs = pl.pallas_call(
        kern,
        grid_spec=grid_spec,
        out_shape=tuple(out_shape) + tuple(c[3] for c in casts),
        compiler_params=pltpu.CompilerParams(
            dimension_semantics=("arbitrary",) * len(grid), vmem_limit_bytes=VMEM_LIMIT_BYTES),
        name=name,
    )(*prefetch, *operands, *[c[0] for c in casts])
    return outs[:n_out], outs[n_out:]


NORM_TM = 1024


def _norm_kernel(x_ref, g_ref, o_ref):
    o_ref[...] = _rms_norm_bf16(x_ref[...], g_ref[...])


def _norm(xf, g, casts):
    T, D = xf.shape
    tm = NORM_TM
    return _call(
        _norm_kernel, "norm_mix", (T // tm,), (xf, g),
        in_specs=[pl.BlockSpec((tm, D), lambda i: (i, 0)), pl.BlockSpec((1, D), lambda i: (0, 0))],
        out_specs=[pl.BlockSpec((tm, D), lambda i: (i, 0))],
        out_shape=[jax.ShapeDtypeStruct((T, D), BF16)],
        casts=casts)


PROJ_TM = 1024
VT_SPAN = 512


def _rope(t, cos, sin_lo, sin_hi):
    return (t * cos + pltpu.roll(t, HEAD_DIM - ROPE_DIM // 2, 1) * sin_lo
            + pltpu.roll(t, ROPE_DIM // 2, 1) * sin_hi)


def _qkv_kernel(h_ref, w_ref, cos_ref, slo_ref, shi_ref, q_ref, k_ref, vt_ref, km_ref):
    tm = PROJ_TM
    h = h_ref[...]
    cos, slo, shi = cos_ref[...], slo_ref[...], shi_ref[...]
    heads_per_chunk = MXU_COLS // HEAD_DIM
    for c in range(ATTN_WIDTH // MXU_COLS):
        csl = slice(c * MXU_COLS, (c + 1) * MXU_COLS)
        qc = jnp.dot(h, w_ref[:, csl], preferred_element_type=F32)
        kc = jnp.dot(h, w_ref[:, ATTN_WIDTH + c * MXU_COLS:ATTN_WIDTH + (c + 1) * MXU_COLS],
                     preferred_element_type=F32)
        vc = jnp.dot(h, w_ref[:, 2 * ATTN_WIDTH + c * MXU_COLS:2 * ATTN_WIDTH + (c + 1) * MXU_COLS],
                     preferred_element_type=F32)
        for r in range(tm // VT_SPAN):
            vt_ref[r, csl, :] = vc[r * VT_SPAN:(r + 1) * VT_SPAN].T.astype(BF16)
        for hh in range(heads_per_chunk):
            sl = slice(hh * HEAD_DIM, (hh + 1) * HEAD_DIM)
            osl = slice(c * MXU_COLS + hh * HEAD_DIM, c * MXU_COLS + (hh + 1) * HEAD_DIM)
            q_ref[:, osl] = (_rope(qc[:, sl], cos, slo, shi) * (ATTN_SCALE * LOG2E)).astype(BF16)
            kf = _rope(kc[:, sl], cos, slo, shi)
            k_ref[:, osl] = kf.astype(BF16)
            for r in range(tm // MOBA_BLOCK):
                km_ref[r, :, osl] = jnp.mean(kf[r * MOBA_BLOCK:(r + 1) * MOBA_BLOCK], axis=0, keepdims=True)


def _qkv_proj(h, w_qkv, cos, slo, shi, seq, casts):
    T, D = h.shape
    tm = PROJ_TM
    tpb = seq // tm
    row = lambda i: (i, 0)
    pos = lambda i: (i % tpb, 0)
    return _call(
        _qkv_kernel, "qkv_proj", (T // tm,), (h, w_qkv, cos, slo, shi),
        in_specs=[
            pl.BlockSpec((tm, D), row),
            _resident((D, 3 * ATTN_WIDTH)),
            pl.BlockSpec((tm, HEAD_DIM), pos),
            pl.BlockSpec((tm, HEAD_DIM), pos),
            pl.BlockSpec((tm, HEAD_DIM), pos),
        ],
        out_specs=[
            pl.BlockSpec((tm, ATTN_WIDTH), row),
            pl.BlockSpec((tm, ATTN_WIDTH), row),
            pl.BlockSpec((tm // VT_SPAN, ATTN_WIDTH, VT_SPAN), lambda i: (i, 0, 0)),
            pl.BlockSpec((tm // MOBA_BLOCK, 1, ATTN_WIDTH), lambda i: (i, 0, 0)),
        ],
        out_shape=[
            jax.ShapeDtypeStruct((T, ATTN_WIDTH), BF16),
            jax.ShapeDtypeStruct((T, ATTN_WIDTH), BF16),
            jax.ShapeDtypeStruct((T // VT_SPAN, ATTN_WIDTH, VT_SPAN), BF16),
            jax.ShapeDtypeStruct((T // MOBA_BLOCK, 1, ATTN_WIDTH), F32),
        ],
        casts=casts)


def _conv_kernel(tiles_per_batch, h_ref, w_ref, cw_ref, z_ref, u_scr):
    tm = PROJ_TM
    i = pl.program_id(0)

    @pl.when(i % tiles_per_batch == 0)
    def _():
        u_scr[0:SUBLANES, :] = jnp.zeros((SUBLANES, CONV_WIDTH), F32)

    h = h_ref[...]
    for c in range(CONV_WIDTH // MXU_COLS):
        csl = slice(c * MXU_COLS, (c + 1) * MXU_COLS)
        cx = jnp.dot(h, w_ref[:, csl], preferred_element_type=F32)
        cb = jnp.dot(h, w_ref[:, CONV_WIDTH + c * MXU_COLS:CONV_WIDTH + (c + 1) * MXU_COLS],
                     preferred_element_type=F32)
        cc = jnp.dot(h, w_ref[:, 2 * CONV_WIDTH + c * MXU_COLS:2 * CONV_WIDTH + (c + 1) * MXU_COLS],
                     preferred_element_type=F32)
        u = cc * cx
        u_scr[SUBLANES:tm + SUBLANES, csl] = u
        conv = (cw_ref[0:1, csl] * u_scr[SUBLANES - 2:tm + SUBLANES - 2, csl]
                + cw_ref[1:2, csl] * u_scr[SUBLANES - 1:tm + SUBLANES - 1, csl]
                + cw_ref[2:3, csl] * u)
        z_ref[:, csl] = (cb * conv).astype(BF16)
        u_scr[0:SUBLANES, csl] = u[tm - SUBLANES:tm]


def _conv_proj(h, w_conv, conv_w, seq, casts):
    T, D = h.shape
    tm = PROJ_TM
    return _call(
        functools.partial(_conv_kernel, seq // tm), "conv_proj", (T // tm,), (h, w_conv, conv_w),
        in_specs=[
            pl.BlockSpec((tm, D), lambda i: (i, 0)),
            _resident((D, 3 * CONV_WIDTH)),
            pl.BlockSpec((CONV_K, CONV_WIDTH), lambda i: (0, 0)),
        ],
        out_specs=[pl.BlockSpec((tm, CONV_WIDTH), lambda i: (i, 0))],
        out_shape=[jax.ShapeDtypeStruct((T, CONV_WIDTH), BF16)],
        scratch_shapes=[pltpu.VMEM((tm + SUBLANES, CONV_WIDTH), F32)],
        casts=casts)


def _gate_kernel(h_ref, wa_ref, wc_ref, o_ref):
    h = h_ref[...]
    for g, w_ref in enumerate((wa_ref, wc_ref)):
        n = w_ref.shape[1]
        for c in range(n // MXU_COLS):
            csl = slice(c * MXU_COLS, (c + 1) * MXU_COLS)
            o_ref[:, g * n + c * MXU_COLS:g * n + (c + 1) * MXU_COLS] = jax.nn.sigmoid(
                jnp.dot(h, w_ref[:, csl], preferred_element_type=F32)).astype(BF16)


def _gate_proj(h, w_ga, w_gc, casts):
    T, D = h.shape
    tm = PROJ_TM
    return _call(
        _gate_kernel, "gate_proj", (T // tm,), (h, w_ga, w_gc),
        in_specs=[pl.BlockSpec((tm, D), lambda i: (i, 0)), _resident((D, D)), _resident((D, D))],
        out_specs=[pl.BlockSpec((tm, 2 * D), lambda i: (i, 0))],
        out_shape=[jax.ShapeDtypeStruct((T, 2 * D), BF16)],
        casts=casts)


MOBA_PAIR = 2
MOBA_UNROLL = 16
MASK_NEG = -1e30
ONES_ROWS = BF16_SUBLANES


def _moba_items(nb):
    qbs, prs = [], []
    for qb in range(nb):
        for pr in range((qb + MOBA_PAIR - 1) // MOBA_PAIR):
            qbs.append(qb)
            prs.append(pr)
    assert len(qbs) % MOBA_UNROLL == 0
    return np.asarray(qbs, np.int32), np.asarray(prs, np.int32), len(qbs) // MOBA_UNROLL


def _moba_kernel(nb, n_trips, tqb_ref, tpr_ref, q_ref, k_ref, vt_ref, km_ref, o_ref,
                 qbias_scr, s_scr, smax_scr, acc_scr, m_scr):
    blk = MOBA_BLOCK
    span = MOBA_PAIR * blk
    vt_blocks = vt_ref.shape[2] // blk
    vt_tiles = MOBA_PAIR // vt_blocks
    ones = jnp.ones((ONES_ROWS, span), BF16)
    lane_row = lax.broadcasted_iota(jnp.int32, (1, HEAD_DIM), 1)

    km = km_ref[...]
    km_hi = km.astype(BF16)
    km_lo = (km - km_hi.astype(F32)).astype(BF16)
    bidx = lax.broadcasted_iota(jnp.int32, (nb, blk), 0)
    causal = (lax.broadcasted_iota(jnp.int32, (blk, blk), 0)
              <= lax.broadcasted_iota(jnp.int32, (blk, blk), 1))

    def choose_and_score(qb):
        rows = slice(qb * blk, (qb + 1) * blk)
        q = q_ref[rows, :]
        gate = (lax.dot_general(km_hi, q, NT_DIMS, preferred_element_type=F32)
                + lax.dot_general(km_lo, q, NT_DIMS, preferred_element_type=F32))
        g = jnp.where(bidx < qb, gate, NEG_INF)
        bias = jnp.full((nb, blk), MASK_NEG, F32)
        for _ in range(MOBA_TOPK):
            top = jnp.max(g, axis=0, keepdims=True)
            first = jnp.min(jnp.where(g == top, bidx, nb), axis=0, keepdims=True)
            hit = bidx == first
            bias = jnp.where(hit & (top > NEG_INF), 0.0, bias)
            g = jnp.where(hit, NEG_INF, g)
        bias = jnp.concatenate([bias, jnp.zeros((HEAD_DIM - nb, blk), F32)], axis=0)
        qbias_scr[rows, :] = bias.T.astype(BF16)
        return lax.dot_general(k_ref[rows, :], q, NT_DIMS, preferred_element_type=F32)

    def own_block(qb, s):
        s = jnp.where(causal, s, NEG_INF)
        m = jnp.max(s, axis=0, keepdims=True)
        p = jnp.exp2(s - m).astype(BF16)
        col = (qb % vt_blocks) * blk
        vt = jnp.concatenate([vt_ref[qb // vt_blocks, :, col:col + blk], ones[:, :blk]], axis=0)
        acc_scr[qb] = jnp.dot(vt, p, preferred_element_type=F32)
        m_scr[qb] = m

    def scores(idx, slot):
        qb, pr = tqb_ref[idx], tpr_ref[idx]
        qrow = pl.multiple_of(qb * blk, blk)
        krow = pl.multiple_of(pr * span, span)
        hot = jnp.concatenate(
            [jnp.broadcast_to(jnp.where(lane_row == pr * MOBA_PAIR + u, 1.0, 0.0).astype(BF16), (blk, HEAD_DIM))
             for u in range(MOBA_PAIR)], axis=0)
        k_ext = jnp.concatenate([k_ref[pl.ds(krow, span), :], hot], axis=1)
        q_ext = jnp.concatenate([q_ref[pl.ds(qrow, blk), :], qbias_scr[pl.ds(qrow, blk), :]], axis=1)
        s = lax.dot_general(k_ext, q_ext, NT_DIMS, preferred_element_type=F32)
        s_scr[slot] = s
        smax_scr[slot] = jnp.max(s.reshape(span // SUBLANES, SUBLANES, blk), axis=0)

    def softmax_pv(idx, slot, m):
        qb, pr = tqb_ref[idx], tpr_ref[idx]
        m = jnp.where(pr == 0, m_scr[qb], m)
        m_new = jnp.maximum(m, jnp.max(smax_scr[slot], axis=0, keepdims=True))
        p = jnp.exp2(s_scr[slot] - m_new).astype(BF16)
        vt = jnp.concatenate([vt_ref[pr * vt_tiles + j] for j in range(vt_tiles)], axis=1)
        vt = jnp.concatenate([vt, ones], axis=0)
        acc_scr[qb] = jnp.exp2(m - m_new) * acc_scr[qb] + jnp.dot(vt, p, preferred_element_type=F32)
        return m_new

    def trip(t, m):
        base = t * MOBA_UNROLL
        for i in range(MOBA_UNROLL):
            m = softmax_pv(base + i, i, m)
            scores(base + MOBA_UNROLL + i, i)
        return m

    own_lag = 2
    scored = [choose_and_score(qb) for qb in range(min(own_lag, nb))]
    for qb in range(nb):
        if qb + own_lag < nb:
            scored.append(choose_and_score(qb + own_lag))
        own_block(qb, scored[qb])
    for i in range(MOBA_UNROLL):
        scores(i, i)

    m = lax.fori_loop(0, n_trips - 1, trip, jnp.zeros((1, blk), F32))
    for i in range(MOBA_UNROLL):
        m = softmax_pv((n_trips - 1) * MOBA_UNROLL + i, i, m)

    for qb in range(nb):
        acc = acc_scr[qb]
        inv = 1.0 / acc[HEAD_DIM:HEAD_DIM + 1]
        o_ref[qb * blk:(qb + 1) * blk, :] = (acc[:HEAD_DIM] * inv).T.astype(BF16)


def _moba_attention(q, k, vt, kmean, batch, seq, casts):
    T = q.shape[0]
    blk = MOBA_BLOCK
    nb = seq // blk
    span = MOBA_PAIR * blk
    vt_span = vt.shape[2]
    assert nb % MOBA_PAIR == 0 and nb <= HEAD_DIM and span % vt_span == 0
    assert vt.shape == (T // vt_span, ATTN_WIDTH, vt_span)
    tqb, tpr, n_trips = _moba_items(nb)
    head = lambda b, h, *_: (b, h)
    return _call(
        functools.partial(_moba_kernel, nb, n_trips), "moba_attention", (batch, N_ATTN_HEADS),
        (q, k, vt, kmean),
        in_specs=[
            pl.BlockSpec((seq, HEAD_DIM), head),
            pl.BlockSpec((seq, HEAD_DIM), head),
            pl.BlockSpec((seq // vt_span, HEAD_DIM, vt_span), lambda b, h, *_: (b, h, 0)),
            pl.BlockSpec((nb, HEAD_DIM), head),
        ],
        out_specs=[pl.BlockSpec((seq, HEAD_DIM), head)],
        out_shape=[jax.ShapeDtypeStruct((T, ATTN_WIDTH), BF16)],
        scratch_shapes=[
            pltpu.VMEM((seq, HEAD_DIM), BF16),
            pltpu.VMEM((MOBA_UNROLL, span, blk), F32),
            pltpu.VMEM((MOBA_UNROLL, SUBLANES, blk), F32),
            pltpu.VMEM((nb, HEAD_DIM + ONES_ROWS, blk), F32),
            pltpu.VMEM((nb, 1, blk), F32),
        ],
        casts=casts,
        prefetch=(jnp.asarray(tqb), jnp.asarray(tpr)))


MIX_TM = 512


def _mix_kernel(a_ref, z_ref, sga_ref, sgc_ref, x_ref, wa_ref, wc_ref, wm_ref, o_ref):
    y_attn = jnp.dot(a_ref[...], wa_ref[...], preferred_element_type=F32)
    y_conv = jnp.dot(z_ref[...], wc_ref[...], preferred_element_type=F32)
    merged = sga_ref[...].astype(F32) * y_attn + sgc_ref[...].astype(F32) * y_conv
    o_ref[...] = x_ref[...] + jnp.dot(merged.astype(BF16), wm_ref[...], preferred_element_type=F32)


def _mix(attn, z, gates, xf, w_attn_out, w_conv_out, w_mix_out, casts):
    T, D = xf.shape
    tm = MIX_TM
    row = lambda i: (i, 0)
    return _call(
        _mix_kernel, "mix", (T // tm,), (attn, z, gates, gates, xf, w_attn_out, w_conv_out, w_mix_out),
        in_specs=[
            pl.BlockSpec((tm, ATTN_WIDTH), row),
            pl.BlockSpec((tm, CONV_WIDTH), row),
            pl.BlockSpec((tm, D), row),
            pl.BlockSpec((tm, D), lambda i: (i, 1)),
            pl.BlockSpec((tm, D), row),
            _resident((ATTN_WIDTH, D)),
            _resident((CONV_WIDTH, D)),
            _resident((D, D)),
        ],
        out_specs=[pl.BlockSpec((tm, D), row)],
        out_shape=[jax.ShapeDtypeStruct((T, D), F32)],
        casts=casts)


def _mem_kv_kernel(mem_ref, g_ref, w_ref, o_ref):
    o_ref[...] = jnp.dot(_rms_norm_bf16(mem_ref[...], g_ref[...]), w_ref[...],
                         preferred_element_type=F32).astype(BF16)


def _mem_kv(memf, g, wkv):
    R, D = memf.shape
    N = wkv.shape[1]
    return _call(
        _mem_kv_kernel, "mem_kv", (1,), (memf, g, wkv),
        in_specs=[pl.BlockSpec((R, D), lambda i: (0, 0)),
                  pl.BlockSpec((1, D), lambda i: (0, 0)),
                  pl.BlockSpec((D, N), lambda i: (0, 0))],
        out_specs=[pl.BlockSpec((R, N), lambda i: (0, 0))],
        out_shape=[jax.ShapeDtypeStruct((R, N), BF16)])


XATTN_TM = 1024


def _xattn_kernel(x_ref, g_ref, wq_ref, kv_ref, wo_ref, o_ref):
    xf = x_ref[...]
    hq = jnp.dot(_rms_norm_bf16(xf, g_ref[...]), wq_ref[...], preferred_element_type=F32)
    hq = (hq * (ATTN_SCALE * LOG2E)).astype(BF16)
    outs = []
    for hh in range(N_XATTN_HEADS):
        sl = slice(hh * HEAD_DIM, (hh + 1) * HEAD_DIM)
        mk = kv_ref[:, sl]
        mv = kv_ref[:, XATTN_WIDTH + hh * HEAD_DIM:XATTN_WIDTH + (hh + 1) * HEAD_DIM]
        s = lax.dot_general(hq[:, sl], mk, NT_DIMS, preferred_element_type=F32)
        p = jnp.exp2(s - jnp.max(s, axis=-1, keepdims=True))
        inv = 1.0 / jnp.sum(p, axis=-1, keepdims=True)
        outs.append((jnp.dot(p.astype(BF16), mv, preferred_element_type=F32) * inv).astype(BF16))
    att = jnp.concatenate(outs, axis=-1)
    o_ref[...] = xf + jnp.dot(att, wo_ref[...], preferred_element_type=F32)


def _xattn(x1, g, wq, kv, wo, seq, n_mem):
    T, D = x1.shape
    tm = XATTN_TM
    tpb = seq // tm
    return _call(
        _xattn_kernel, "xattn", (T // tm,), (x1, g, wq, kv, wo),
        in_specs=[
            pl.BlockSpec((tm, D), lambda i: (i, 0)),
            pl.BlockSpec((1, D), lambda i: (0, 0)),
            _resident((D, XATTN_WIDTH)),
            pl.BlockSpec((n_mem, 2 * XATTN_WIDTH), lambda i: (i // tpb, 0)),
            _resident((XATTN_WIDTH, D)),
        ],
        out_specs=[pl.BlockSpec((tm, D), lambda i: (i, 0))],
        out_shape=[jax.ShapeDtypeStruct((T, D), F32)])


MLP_TM = 512
MLP_TF = 1024


def _mlp_kernel(x_ref, g_ref, wu_ref, wd_ref, gf_ref, o_ref, h_scr, acc_scr):
    f = pl.program_id(1)

    @pl.when(f == 0)
    def _():
        xf = x_ref[...]
        h_scr[...] = _rms_norm_bf16(xf, g_ref[...])
        acc_scr[...] = xf

    u = jnp.maximum(jnp.dot(h_scr[...], wu_ref[...], preferred_element_type=F32), 0.0)
    acc_scr[...] += jnp.dot((u * u).astype(BF16), wd_ref[...], preferred_element_type=F32)

    @pl.when(f == pl.num_programs(1) - 1)
    def _():
        y = acc_scr[...]
        ms = jnp.mean(y * y, axis=-1, keepdims=True)
        o_ref[...] = y * lax.rsqrt(ms + NORM_EPS) * gf_ref[...]


def _mlp(x2, g, w_up, w_down, g_final):
    T, D = x2.shape
    FF = w_up.shape[1]
    tm, tf = MLP_TM, MLP_TF
    return _call(
        _mlp_kernel, "mlp", (T // tm, FF // tf), (x2, g, w_up, w_down, g_final),
        in_specs=[
            pl.BlockSpec((tm, D), lambda i, f: (i, 0)),
            pl.BlockSpec((1, D), lambda i, f: (0, 0)),
            pl.BlockSpec((D, tf), lambda i, f: (0, f)),
            pl.BlockSpec((tf, D), lambda i, f: (f, 0)),
            pl.BlockSpec((1, D), lambda i, f: (0, 0)),
        ],
        out_specs=[pl.BlockSpec((tm, D), lambda i, f: (i, 0))],
        out_shape=[jax.ShapeDtypeStruct((T, D), F32)],
        scratch_shapes=[pltpu.VMEM((tm, D), BF16), pltpu.VMEM((tm, D), F32)])


def _rope_tables(seq):
    half = ROPE_DIM // 2
    inv_freq = ROPE_THETA ** (-np.arange(half, dtype=np.float64) / half)
    ang = np.arange(seq, dtype=np.float64)[:, None] * inv_freq[None, :]
    cos, sin = np.cos(ang), np.sin(ang)
    pad = HEAD_DIM - ROPE_DIM
    cos_t = np.concatenate([cos, cos, np.ones((seq, pad))], axis=-1)
    sin_lo = np.concatenate([-sin, np.zeros((seq, half + pad))], axis=-1)
    sin_hi = np.concatenate([np.zeros((seq, half)), sin, np.zeros((seq, pad))], axis=-1)
    return tuple(jnp.asarray(t, F32) for t in (cos_t, sin_lo, sin_hi))


def _layer(xf, memf, batch, seq, n_mem, norm_mix, w_in, conv_w, w_attn_out, w_conv_out, w_mix_out,
           norm_xattn, norm_mem, wq_x, wkv_x, wo_x, norm_mlp, w_up, w_down, g_out):
    T, D = xf.shape
    row = lambda g: g.reshape(1, D).astype(F32)
    cos_t, sin_lo, sin_hi = _rope_tables(seq)
    n_qkv, n_conv = 3 * ATTN_WIDTH, 3 * CONV_WIDTH
    assert n_qkv == n_conv and (n_qkv + n_conv) % D == 0
    proj_steps = T // PROJ_TM
    head_step = lambda b, h, *_: b * N_ATTN_HEADS + h

    (h,), (w_qkv,) = _norm(xf, row(norm_mix), [_cast_job(w_in, T // NORM_TM, n_qkv, 0)])
    (q, k, vt, kmean), (w_conv, w_ga, w_gc) = _qkv_proj(
        h, w_qkv, cos_t, sin_lo, sin_hi, seq,
        [_cast_job(w_in, proj_steps, n_conv, 1),
         _cast_job(w_in, proj_steps, D, (n_qkv + n_conv) // D),
         _cast_job(w_in, proj_steps, D, (n_qkv + n_conv) // D + 1)])
    (z,), (wa, wc, wm) = _conv_proj(
        h, w_conv, conv_w, seq,
        [_cast_job(w_attn_out, proj_steps), _cast_job(w_conv_out, proj_steps), _cast_job(w_mix_out, proj_steps)])
    (gates,), (wq, wkv, wo) = _gate_proj(
        h, w_ga, w_gc, [_cast_job(wq_x, proj_steps), _cast_job(wkv_x, proj_steps), _cast_job(wo_x, proj_steps)])
    kmean = kmean.reshape(T // MOBA_BLOCK, ATTN_WIDTH)
    (attn,), (wu,) = _moba_attention(
        q, k, vt, kmean, batch, seq, [_cast_job(w_up, batch * N_ATTN_HEADS, step_of=head_step)])
    (x1,), (wd,) = _mix(attn, z, gates, xf, wa, wc, wm, [_cast_job(w_down, T // MIX_TM)])
    (kv,), _ = _mem_kv(memf, row(norm_mem), wkv)
    (x2,), _ = _xattn(x1, row(norm_xattn), wq, kv, wo, seq, n_mem)
    (out,), _ = _mlp(x2, row(norm_mlp), wu, wd, g_out)
    return out


def kernel(x, mem, norm_mix, w_in, conv_w, w_attn_out, w_conv_out, w_mix_out, norm_xattn, norm_mem,
           wq_x, wkv_x, wo_x, norm_mlp, w_up, w_down, norm_final):
    B, S, D = x.shape
    n_mem = mem.shape[1]
    depth = w_in.shape[0]
    assert depth == 1, "the final rmsnorm is fused into the last layer's MLP kernel"
    xf = x.reshape(B * S, D)
    memf = mem.reshape(B * n_mem, D)
    out = _layer(xf, memf, B, S, n_mem, norm_mix[0], w_in[0], conv_w[0], w_attn_out[0], w_conv_out[0],
                 w_mix_out[0], norm_xattn[0], norm_mem[0], wq_x[0], wkv_x[0], wo_x[0], norm_mlp[0],
                 w_up[0], w_down[0], norm_final.reshape(1, D).astype(F32))
    return out.reshape(B, S, D)
```

```python
import functools
import math

import jax
import jax.numpy as jnp
import numpy as np
from jax import lax
from jax.experimental import pallas as pl
from jax.experimental.pallas import tpu as pltpu

F32 = jnp.float32
BF16 = jnp.bfloat16

HEAD_DIM = 128
N_ATTN_HEADS = 8
ATTN_WIDTH = N_ATTN_HEADS * HEAD_DIM
CONV_WIDTH = 1024
CONV_K = 3
MOBA_BLOCK = 256
MOBA_TOPK = 3
ROPE_THETA = 500000.0
ROPE_DIM = HEAD_DIM // 4
N_XATTN_HEADS = 4
XATTN_WIDTH = N_XATTN_HEADS * HEAD_DIM
NORM_EPS = 1e-6

LOG2E = math.log2(math.e)
ATTN_SCALE = HEAD_DIM ** -0.5
NEG_INF = float("-inf")

VMEM_LIMIT_BYTES = 58 * 1024 * 1024
SUBLANES = 8
BF16_SUBLANES = 16
MXU_COLS = 256

NT_DIMS = (((1,), (1,)), ((), ()))


def _rms_norm_bf16(xf, g):
    ms = jnp.mean(xf * xf, axis=-1, keepdims=True)
    return (xf * lax.rsqrt(ms + NORM_EPS) * g).astype(BF16)


def _resident(shape):
    return pl.BlockSpec(shape, lambda *_: (0,) * len(shape), pipeline_mode=pl.Buffered(1))


def _cast_job(w, steps, n_cols=None, col_block=0, step_of=lambda i, *_: i):
    n_rows = w.shape[0]
    n_cols = w.shape[1] if n_cols is None else n_cols
    rows = n_rows // steps
    assert rows * steps == n_rows and rows % BF16_SUBLANES == 0
    in_spec = pl.BlockSpec((rows, n_cols), lambda *g: (step_of(*g), col_block))
    out_spec = pl.BlockSpec((rows, n_cols), lambda *g: (step_of(*g), 0))
    return w, in_spec, out_spec, jax.ShapeDtypeStruct((n_rows, n_cols), BF16)


def _call(body, name, grid, operands, in_specs, out_specs, out_shape, scratch_shapes=(), casts=(),
          prefetch=()):
    n_in = len(prefetch) + len(operands)
    n_out = len(out_shape)
    n_cast = len(casts)

    def kern(*refs):
        ins, refs = refs[:n_in], refs[n_in:]
        cast_src, refs = refs[:n_cast], refs[n_cast:]
        outs, refs = refs[:n_out], refs[n_out:]
        cast_dst, scratch = refs[:n_cast], refs[n_cast:]
        body(*ins, *outs, *scratch)
        for src, dst in zip(cast_src, cast_dst):
            dst[...] = src[...].astype(BF16)

    grid_spec = pltpu.PrefetchScalarGridSpec(
        num_scalar_prefetch=len(prefetch),
        grid=grid,
        in_specs=list(in_specs) + [c[1] for c in casts],
        out_specs=tuple(out_specs) + tuple(c[2] for c in casts),
        scratch_shapes=list(scratch_shapes),
    )
    outs = pl.pallas_call(
        kern,
        grid_spec=grid_spec,
        out_shape=tuple(out_shape) + tuple(c[3] for c in casts),
        compiler_params=pltpu.CompilerParams(
            dimension_semantics=("arbitrary",) * len(grid), vmem_limit_bytes=VMEM_LIMIT_BYTES),
        name=name,
    )(*prefetch, *operands, *[c[0] for c in casts])
    return outs[:n_out], outs[n_out:]


NORM_TM = 1024


def _norm_kernel(x_ref, g_ref, o_ref):
    o_ref[...] = _rms_norm_bf16(x_ref[...], g_ref[...])


def _norm(xf, g, casts):
    T, D = xf.shape
    tm = NORM_TM
    return _call(
        _norm_kernel, "norm_mix", (T // tm,), (xf, g),
        in_specs=[pl.BlockSpec((tm, D), lambda i: (i, 0)), pl.BlockSpec((1, D), lambda i: (0, 0))],
        out_specs=[pl.BlockSpec((tm, D), lambda i: (i, 0))],
        out_shape=[jax.ShapeDtypeStruct((T, D), BF16)],
        casts=casts)


PROJ_TM = 1024
VT_SPAN = 512


def _rope(t, cos, sin_lo, sin_hi):
    return (t * cos + pltpu.roll(t, HEAD_DIM - ROPE_DIM // 2, 1) * sin_lo
            + pltpu.roll(t, ROPE_DIM // 2, 1) * sin_hi)


def _qkv_kernel(h_ref, w_ref, cos_ref, slo_ref, shi_ref, q_ref, k_ref, vt_ref, km_ref):
    tm = PROJ_TM
    h = h_ref[...]
    cos, slo, shi = cos_ref[...], slo_ref[...], shi_ref[...]
    heads_per_chunk = MXU_COLS // HEAD_DIM
    for c in range(ATTN_WIDTH // MXU_COLS):
        csl = slice(c * MXU_COLS, (c + 1) * MXU_COLS)
        qc = jnp.dot(h, w_ref[:, csl], preferred_element_type=F32)
        kc = jnp.dot(h, w_ref[:, ATTN_WIDTH + c * MXU_COLS:ATTN_WIDTH + (c + 1) * MXU_COLS],
                     preferred_element_type=F32)
        vc = jnp.dot(h, w_ref[:, 2 * ATTN_WIDTH + c * MXU_COLS:2 * ATTN_WIDTH + (c + 1) * MXU_COLS],
                     preferred_element_type=F32)
        for r in range(tm // VT_SPAN):
            vt_ref[r, csl, :] = vc[r * VT_SPAN:(r + 1) * VT_SPAN].T.astype(BF16)
        for hh in range(heads_per_chunk):
            sl = slice(hh * HEAD_DIM, (hh + 1) * HEAD_DIM)
            osl = slice(c * MXU_COLS + hh * HEAD_DIM, c * MXU_COLS + (hh + 1) * HEAD_DIM)
            q_ref[:, osl] = (_rope(qc[:, sl], cos, slo, shi) * (ATTN_SCALE * LOG2E)).astype(BF16)
            kf = _rope(kc[:, sl], cos, slo, shi)
            k_ref[:, osl] = kf.astype(BF16)
            for r in range(tm // MOBA_BLOCK):
                km_ref[r, :, osl] = jnp.mean(kf[r * MOBA_BLOCK:(r + 1) * MOBA_BLOCK], axis=0, keepdims=True)


def _qkv_proj(h, w_qkv, cos, slo, shi, seq, casts):
    T, D = h.shape
    tm = PROJ_TM
    tpb = seq // tm
    row = lambda i: (i, 0)
    pos = lambda i: (i % tpb, 0)
    return _call(
        _qkv_kernel, "qkv_proj", (T // tm,), (h, w_qkv, cos, slo, shi),
        in_specs=[
            pl.BlockSpec((tm, D), row),
            _resident((D, 3 * ATTN_WIDTH)),
            pl.BlockSpec((tm, HEAD_DIM), pos),
            pl.BlockSpec((tm, HEAD_DIM), pos),
            pl.BlockSpec((tm, HEAD_DIM), pos),
        ],
        out_specs=[
            pl.BlockSpec((tm, ATTN_WIDTH), row),
            pl.BlockSpec((tm, ATTN_WIDTH), row),
            pl.BlockSpec((tm // VT_SPAN, ATTN_WIDTH, VT_SPAN), lambda i: (i, 0, 0)),
            pl.BlockSpec((tm // MOBA_BLOCK, 1, ATTN_WIDTH), lambda i: (i, 0, 0)),
        ],
        out_shape=[
            jax.ShapeDtypeStruct((T, ATTN_WIDTH), BF16),
            jax.ShapeDtypeStruct((T, ATTN_WIDTH), BF16),
            jax.ShapeDtypeStruct((T // VT_SPAN, ATTN_WIDTH, VT_SPAN), BF16),
            jax.ShapeDtypeStruct((T // MOBA_BLOCK, 1, ATTN_WIDTH), F32),
        ],
        casts=casts)


def _conv_kernel(tiles_per_batch, h_ref, w_ref, cw_ref, z_ref, u_scr):
    tm = PROJ_TM
    i = pl.program_id(0)

    @pl.when(i % tiles_per_batch == 0)
    def _():
        u_scr[0:SUBLANES, :] = jnp.zeros((SUBLANES, CONV_WIDTH), F32)

    h = h_ref[...]
    for c in range(CONV_WIDTH // MXU_COLS):
        csl = slice(c * MXU_COLS, (c + 1) * MXU_COLS)
        cx = jnp.dot(h, w_ref[:, csl], preferred_element_type=F32)
        cb = jnp.dot(h, w_ref[:, CONV_WIDTH + c * MXU_COLS:CONV_WIDTH + (c + 1) * MXU_COLS],
                     preferred_element_type=F32)
        cc = jnp.dot(h, w_ref[:, 2 * CONV_WIDTH + c * MXU_COLS:2 * CONV_WIDTH + (c + 1) * MXU_COLS],
                     preferred_element_type=F32)
        u = cc * cx
        u_scr[SUBLANES:tm + SUBLANES, csl] = u
        conv = (cw_ref[0:1, csl] * u_scr[SUBLANES - 2:tm + SUBLANES - 2, csl]
                + cw_ref[1:2, csl] * u_scr[SUBLANES - 1:tm + SUBLANES - 1, csl]
                + cw_ref[2:3, csl] * u)
        z_ref[:, csl] = (cb * conv).astype(BF16)
        u_scr[0:SUBLANES, csl] = u[tm - SUBLANES:tm]


def _conv_proj(h, w_conv, conv_w, seq, casts):
    T, D = h.shape
    tm = PROJ_TM
    return _call(
        functools.partial(_conv_kernel, seq // tm), "conv_proj", (T // tm,), (h, w_conv, conv_w),
        in_specs=[
            pl.BlockSpec((tm, D), lambda i: (i, 0)),
            _resident((D, 3 * CONV_WIDTH)),
            pl.BlockSpec((CONV_K, CONV_WIDTH), lambda i: (0, 0)),
        ],
        out_specs=[pl.BlockSpec((tm, CONV_WIDTH), lambda i: (i, 0))],
        out_shape=[jax.ShapeDtypeStruct((T, CONV_WIDTH), BF16)],
        scratch_shapes=[pltpu.VMEM((tm + SUBLANES, CONV_WIDTH), F32)],
        casts=casts)


def _gate_kernel(h_ref, wa_ref, wc_ref, o_ref):
    h = h_ref[...]
    for g, w_ref in enumerate((wa_ref, wc_ref)):
        n = w_ref.shape[1]
        for c in range(n // MXU_COLS):
            csl = slice(c * MXU_COLS, (c + 1) * MXU_COLS)
            o_ref[:, g * n + c * MXU_COLS:g * n + (c + 1) * MXU_COLS] = jax.nn.sigmoid(
                jnp.dot(h, w_ref[:, csl], preferred_element_type=F32)).astype(BF16)


def _gate_proj(h, w_ga, w_gc, casts):
    T, D = h.shape
    tm = PROJ_TM
    return _call(
        _gate_kernel, "gate_proj", (T // tm,), (h, w_ga, w_gc),
        in_specs=[pl.BlockSpec((tm, D), lambda i: (i, 0)), _resident((D, D)), _resident((D, D))],
        out_specs=[pl.BlockSpec((tm, 2 * D), lambda i: (i, 0))],
        out_shape=[jax.ShapeDtypeStruct((T, 2 * D), BF16)],
        casts=casts)


MOBA_PAIR = 2
MOBA_UNROLL = 32
MASK_NEG = -1e30
ONES_ROWS = BF16_SUBLANES


def _moba_items(nb):
    qbs, prs = [], []
    for qb in range(nb):
        for pr in range((qb + MOBA_PAIR - 1) // MOBA_PAIR):
            qbs.append(qb)
            prs.append(pr)
    assert len(qbs) % MOBA_UNROLL == 0
    return np.asarray(qbs, np.int32), np.asarray(prs, np.int32), len(qbs) // MOBA_UNROLL


def _moba_kernel(nb, n_trips, tqb_ref, tpr_ref, q_ref, k_ref, vt_ref, km_ref, o_ref,
                 qbias_scr, s_scr, smax_scr, acc_scr, m_scr):
    blk = MOBA_BLOCK
    span = MOBA_PAIR * blk
    vt_blocks = vt_ref.shape[2] // blk
    vt_tiles = MOBA_PAIR // vt_blocks
    ones = jnp.ones((ONES_ROWS, span), BF16)
    lane_row = lax.broadcasted_iota(jnp.int32, (1, HEAD_DIM), 1)

    km = km_ref[...]
    km_hi = km.astype(BF16)
    km_lo = (km - km_hi.astype(F32)).astype(BF16)
    bidx = lax.broadcasted_iota(jnp.int32, (nb, blk), 0)
    causal = (lax.broadcasted_iota(jnp.int32, (blk, blk), 0)
              <= lax.broadcasted_iota(jnp.int32, (blk, blk), 1))

    def choose_and_score(qb):
        rows = slice(qb * blk, (qb + 1) * blk)
        q = q_ref[rows, :]
        gate = (lax.dot_general(km_hi, q, NT_DIMS, preferred_element_type=F32)
                + lax.dot_general(km_lo, q, NT_DIMS, preferred_element_type=F32))
        g = jnp.where(bidx < qb, gate, NEG_INF)
        bias = jnp.full((nb, blk), MASK_NEG, F32)
        for _ in range(MOBA_TOPK):
            top = jnp.max(g, axis=0, keepdims=True)
            first = jnp.min(jnp.where(g == top, bidx, nb), axis=0, keepdims=True)
            hit = bidx == first
            bias = jnp.where(hit & (top > NEG_INF), 0.0, bias)
            g = jnp.where(hit, NEG_INF, g)
        bias = jnp.concatenate([bias, jnp.zeros((HEAD_DIM - nb, blk), F32)], axis=0)
        qbias_scr[rows, :] = bias.T.astype(BF16)
        return lax.dot_general(k_ref[rows, :], q, NT_DIMS, preferred_element_type=F32)

    def own_block(qb, s):
        s = jnp.where(causal, s, NEG_INF)
        m = jnp.max(s, axis=0, keepdims=True)
        p = jnp.exp2(s - m).astype(BF16)
        col = (qb % vt_blocks) * blk
        vt = jnp.concatenate([vt_ref[qb // vt_blocks, :, col:col + blk], ones[:, :blk]], axis=0)
        acc_scr[qb] = jnp.dot(vt, p, preferred_element_type=F32)
        m_scr[qb] = m

    def scores(idx, slot):
        qb, pr = tqb_ref[idx], tpr_ref[idx]
        qrow = pl.multiple_of(qb * blk, blk)
        krow = pl.multiple_of(pr * span, span)
        hot = jnp.concatenate(
            [jnp.broadcast_to(jnp.where(lane_row == pr * MOBA_PAIR + u, 1.0, 0.0).astype(BF16), (blk, HEAD_DIM))
             for u in range(MOBA_PAIR)], axis=0)
        k_ext = jnp.concatenate([k_ref[pl.ds(krow, span), :], hot], axis=1)
        q_ext = jnp.concatenate([q_ref[pl.ds(qrow, blk), :], qbias_scr[pl.ds(qrow, blk), :]], axis=1)
        s = lax.dot_general(k_ext, q_ext, NT_DIMS, preferred_element_type=F32)
        s_scr[slot] = s
        smax_scr[slot] = jnp.max(s.reshape(span // SUBLANES, SUBLANES, blk), axis=0)

    def softmax_pv(idx, slot, m):
        qb, pr = tqb_ref[idx], tpr_ref[idx]
        m = jnp.where(pr == 0, m_scr[qb], m)
        m_new = jnp.maximum(m, jnp.max(smax_scr[slot], axis=0, keepdims=True))
        p = jnp.exp2(s_scr[slot] - m_new).astype(BF16)
        vt = jnp.concatenate([vt_ref[pr * vt_tiles + j] for j in range(vt_tiles)], axis=1)
        vt = jnp.concatenate([vt, ones], axis=0)
        acc_scr[qb] = jnp.exp2(m - m_new) * acc_scr[qb] + jnp.dot(vt, p, preferred_element_type=F32)
        return m_new

    def trip(t, m):
        base = t * MOBA_UNROLL
        for i in range(MOBA_UNROLL):
            m = softmax_pv(base + i, i, m)
            scores(base + MOBA_UNROLL + i, i)
        return m

    own_lag = 2
    scored = [choose_and_score(qb) for qb in range(min(own_lag, nb))]
    for qb in range(nb):
        if qb + own_lag < nb:
            scored.append(choose_and_score(qb + own_lag))
        own_block(qb, scored[qb])
    for i in range(MOBA_UNROLL):
        scores(i, i)

    m = lax.fori_loop(0, n_trips - 1, trip, jnp.zeros((1, blk), F32))
    for i in range(MOBA_UNROLL):
        m = softmax_pv((n_trips - 1) * MOBA_UNROLL + i, i, m)

    for qb in range(nb):
        acc = acc_scr[qb]
        inv = 1.0 / acc[HEAD_DIM:HEAD_DIM + 1]
        o_ref[qb * blk:(qb + 1) * blk, :] = (acc[:HEAD_DIM] * inv).T.astype(BF16)


def _moba_attention(q, k, vt, kmean, batch, seq, casts):
    T = q.shape[0]
    blk = MOBA_BLOCK
    nb = seq // blk
    span = MOBA_PAIR * blk
    vt_span = vt.shape[2]
    assert nb % MOBA_PAIR == 0 and nb <= HEAD_DIM and span % vt_span == 0
    assert vt.shape == (T // vt_span, ATTN_WIDTH, vt_span)
    tqb, tpr, n_trips = _moba_items(nb)
    head = lambda b, h, *_: (b, h)
    return _call(
        functools.partial(_moba_kernel, nb, n_trips), "moba_attention", (batch, N_ATTN_HEADS),
        (q, k, vt, kmean),
        in_specs=[
            pl.BlockSpec((seq, HEAD_DIM), head),
            pl.BlockSpec((seq, HEAD_DIM), head),
            pl.BlockSpec((seq // vt_span, HEAD_DIM, vt_span), lambda b, h, *_: (b, h, 0)),
            pl.BlockSpec((nb, HEAD_DIM), head),
        ],
        out_specs=[pl.BlockSpec((seq, HEAD_DIM), head)],
        out_shape=[jax.ShapeDtypeStruct((T, ATTN_WIDTH), BF16)],
        scratch_shapes=[
            pltpu.VMEM((seq, HEAD_DIM), BF16),
            pltpu.VMEM((MOBA_UNROLL, span, blk), F32),
            pltpu.VMEM((MOBA_UNROLL, SUBLANES, blk), F32),
            pltpu.VMEM((nb, HEAD_DIM + ONES_ROWS, blk), F32),
            pltpu.VMEM((nb, 1, blk), F32),
        ],
        casts=casts,
        prefetch=(jnp.asarray(tqb), jnp.asarray(tpr)))


MIX_TM = 512


def _mix_kernel(a_ref, z_ref, sga_ref, sgc_ref, x_ref, wa_ref, wc_ref, wm_ref, o_ref):
    y_attn = jnp.dot(a_ref[...], wa_ref[...], preferred_element_type=F32)
    y_conv = jnp.dot(z_ref[...], wc_ref[...], preferred_element_type=F32)
    merged = sga_ref[...].astype(F32) * y_attn + sgc_ref[...].astype(F32) * y_conv
    o_ref[...] = x_ref[...] + jnp.dot(merged.astype(BF16), wm_ref[...], preferred_element_type=F32)


def _mix(attn, z, gates, xf, w_attn_out, w_conv_out, w_mix_out, casts):
    T, D = xf.shape
    tm = MIX_TM
    row = lambda i: (i, 0)
    return _call(
        _mix_kernel, "mix", (T // tm,), (attn, z, gates, gates, xf, w_attn_out, w_conv_out, w_mix_out),
        in_specs=[
            pl.BlockSpec((tm, ATTN_WIDTH), row),
            pl.BlockSpec((tm, CONV_WIDTH), row),
            pl.BlockSpec((tm, D), row),
            pl.BlockSpec((tm, D), lambda i: (i, 1)),
            pl.BlockSpec((tm, D), row),
            _resident((ATTN_WIDTH, D)),
            _resident((CONV_WIDTH, D)),
            _resident((D, D)),
        ],
        out_specs=[pl.BlockSpec((tm, D), row)],
        out_shape=[jax.ShapeDtypeStruct((T, D), F32)],
        casts=casts)


def _mem_kv_kernel(mem_ref, g_ref, w_ref, o_ref):
    o_ref[...] = jnp.dot(_rms_norm_bf16(mem_ref[...], g_ref[...]), w_ref[...],
                         preferred_element_type=F32).astype(BF16)


def _mem_kv(memf, g, wkv):
    R, D = memf.shape
    N = wkv.shape[1]
    return _call(
        _mem_kv_kernel, "mem_kv", (1,), (memf, g, wkv),
        in_specs=[pl.BlockSpec((R, D), lambda i: (0, 0)),
                  pl.BlockSpec((1, D), lambda i: (0, 0)),
                  pl.BlockSpec((D, N), lambda i: (0, 0))],
        out_specs=[pl.BlockSpec((R, N), lambda i: (0, 0))],
        out_shape=[jax.ShapeDtypeStruct((R, N), BF16)])


XATTN_TM = 1024


def _xattn_kernel(x_ref, g_ref, wq_ref, kv_ref, wo_ref, o_ref):
    xf = x_ref[...]
    hq = jnp.dot(_rms_norm_bf16(xf, g_ref[...]), wq_ref[...], preferred_element_type=F32)
    hq = (hq * (ATTN_SCALE * LOG2E)).astype(BF16)
    outs = []
    for hh in range(N_XATTN_HEADS):
        sl = slice(hh * HEAD_DIM, (hh + 1) * HEAD_DIM)
        mk = kv_ref[:, sl]
        mv = kv_ref[:, XATTN_WIDTH + hh * HEAD_DIM:XATTN_WIDTH + (hh + 1) * HEAD_DIM]
        s = lax.dot_general(hq[:, sl], mk, NT_DIMS, preferred_element_type=F32)
        p = jnp.exp2(s - jnp.max(s, axis=-1, keepdims=True))
        inv = 1.0 / jnp.sum(p, axis=-1, keepdims=True)
        outs.append((jnp.dot(p.astype(BF16), mv, preferred_element_type=F32) * inv).astype(BF16))
    att = jnp.concatenate(outs, axis=-1)
    o_ref[...] = xf + jnp.dot(att, wo_ref[...], preferred_element_type=F32)


def _xattn(x1, g, wq, kv, wo, seq, n_mem):
    T, D = x1.shape
    tm = XATTN_TM
    tpb = seq // tm
    return _call(
        _xattn_kernel, "xattn", (T // tm,), (x1, g, wq, kv, wo),
        in_specs=[
            pl.BlockSpec((tm, D), lambda i: (i, 0)),
            pl.BlockSpec((1, D), lambda i: (0, 0)),
            _resident((D, XATTN_WIDTH)),
            pl.BlockSpec((n_mem, 2 * XATTN_WIDTH), lambda i: (i // tpb, 0)),
            _resident((XATTN_WIDTH, D)),
        ],
        out_specs=[pl.BlockSpec((tm, D), lambda i: (i, 0))],
        out_shape=[jax.ShapeDtypeStruct((T, D), F32)])


MLP_TM = 512
MLP_TF = 1024


def _mlp_kernel(x_ref, g_ref, wu_ref, wd_ref, gf_ref, o_ref, h_scr, acc_scr):
    f = pl.program_id(1)

    @pl.when(f == 0)
    def _():
        xf = x_ref[...]
        h_scr[...] = _rms_norm_bf16(xf, g_ref[...])
        acc_scr[...] = xf

    u = jnp.maximum(jnp.dot(h_scr[...], wu_ref[...], preferred_element_type=F32), 0.0)
    acc_scr[...] += jnp.dot((u * u).astype(BF16), wd_ref[...], preferred_element_type=F32)

    @pl.when(f == pl.num_programs(1) - 1)
    def _():
        y = acc_scr[...]
        ms = jnp.mean(y * y, axis=-1, keepdims=True)
        o_ref[...] = y * lax.rsqrt(ms + NORM_EPS) * gf_ref[...]


def _mlp(x2, g, w_up, w_down, g_final):
    T, D = x2.shape
    FF = w_up.shape[1]
    tm, tf = MLP_TM, MLP_TF
    return _call(
        _mlp_kernel, "mlp", (T // tm, FF // tf), (x2, g, w_up, w_down, g_final),
        in_specs=[
            pl.BlockSpec((tm, D), lambda i, f: (i, 0)),
            pl.BlockSpec((1, D), lambda i, f: (0, 0)),
            pl.BlockSpec((D, tf), lambda i, f: (0, f)),
            pl.BlockSpec((tf, D), lambda i, f: (f, 0)),
            pl.BlockSpec((1, D), lambda i, f: (0, 0)),
        ],
        out_specs=[pl.BlockSpec((tm, D), lambda i, f: (i, 0))],
        out_shape=[jax.ShapeDtypeStruct((T, D), F32)],
        scratch_shapes=[pltpu.VMEM((tm, D), BF16), pltpu.VMEM((tm, D), F32)])


def _rope_tables(seq):
    half = ROPE_DIM // 2
    inv_freq = ROPE_THETA ** (-np.arange(half, dtype=np.float64) / half)
    ang = np.arange(seq, dtype=np.float64)[:, None] * inv_freq[None, :]
    cos, sin = np.cos(ang), np.sin(ang)
    pad = HEAD_DIM - ROPE_DIM
    cos_t = np.concatenate([cos, cos, np.ones((seq, pad))], axis=-1)
    sin_lo = np.concatenate([-sin, np.zeros((seq, half + pad))], axis=-1)
    sin_hi = np.concatenate([np.zeros((seq, half)), sin, np.zeros((seq, pad))], axis=-1)
    return tuple(jnp.asarray(t, F32) for t in (cos_t, sin_lo, sin_hi))


def _layer(xf, memf, batch, seq, n_mem, norm_mix, w_in, conv_w, w_attn_out, w_conv_out, w_mix_out,
           norm_xattn, norm_mem, wq_x, wkv_x, wo_x, norm_mlp, w_up, w_down, g_out):
    T, D = xf.shape
    row = lambda g: g.reshape(1, D).astype(F32)
    cos_t, sin_lo, sin_hi = _rope_tables(seq)
    n_qkv, n_conv = 3 * ATTN_WIDTH, 3 * CONV_WIDTH
    assert n_qkv == n_conv and (n_qkv + n_conv) % D == 0
    proj_steps = T // PROJ_TM
    head_step = lambda b, h, *_: b * N_ATTN_HEADS + h

    (h,), (w_qkv,) = _norm(xf, row(norm_mix), [_cast_job(w_in, T // NORM_TM, n_qkv, 0)])
    (q, k, vt, kmean), (w_conv, w_ga, w_gc) = _qkv_proj(
        h, w_qkv, cos_t, sin_lo, sin_hi, seq,
        [_cast_job(w_in, proj_steps, n_conv, 1),
         _cast_job(w_in, proj_steps, D, (n_qkv + n_conv) // D),
         _cast_job(w_in, proj_steps, D, (n_qkv + n_conv) // D + 1)])
    (z,), (wa, wc, wm) = _conv_proj(
        h, w_conv, conv_w, seq,
        [_cast_job(w_attn_out, proj_steps), _cast_job(w_conv_out, proj_steps), _cast_job(w_mix_out, proj_steps)])
    (gates,), (wq, wkv, wo) = _gate_proj(
        h, w_ga, w_gc, [_cast_job(wq_x, proj_steps), _cast_job(wkv_x, proj_steps), _cast_job(wo_x, proj_steps)])
    kmean = kmean.reshape(T // MOBA_BLOCK, ATTN_WIDTH)
    (attn,), (wu,) = _moba_attention(
        q, k, vt, kmean, batch, seq, [_cast_job(w_up, batch * N_ATTN_HEADS, step_of=head_step)])
    (x1,), (wd,) = _mix(attn, z, gates, xf, wa, wc, wm, [_cast_job(w_down, T // MIX_TM)])
    (kv,), _ = _mem_kv(memf, row(norm_mem), wkv)
    (x2,), _ = _xattn(x1, row(norm_xattn), wq, kv, wo, seq, n_mem)
    (out,), _ = _mlp(x2, row(norm_mlp), wu, wd, g_out)
    return out


def kernel(x, mem, norm_mix, w_in, conv_w, w_attn_out, w_conv_out, w_mix_out, norm_xattn, norm_mem,
           wq_x, wkv_x, wo_x, norm_mlp, w_up, w_down, norm_final):
    B, S, D = x.shape
    n_mem = mem.shape[1]
    depth = w_in.shape[0]
    assert depth == 1, "the final rmsnorm is fused into the last layer's MLP kernel"
    xf = x.reshape(B * S, D)
    memf = mem.reshape(B * n_mem, D)
    out = _layer(xf, memf, B, S, n_mem, norm_mix[0], w_in[0], conv_w[0], w_attn_out[0], w_conv_out[0],
                 w_mix_out[0], norm_xattn[0], norm_mem[0], wq_x[0], wkv_x[0], wo_x[0], norm_mlp[0],
                 w_up[0], w_down[0], norm_final.reshape(1, D).astype(F32))
    return out.reshape(B, S, D)
```

```python
import functools
import math

import jax
import jax.numpy as jnp
import numpy as np
from jax import lax
from jax.experimental import pallas as pl
from jax.experimental.pallas import tpu as pltpu

F32 = jnp.float32
BF16 = jnp.bfloat16

HEAD_DIM = 128
N_ATTN_HEADS = 8
ATTN_WIDTH = N_ATTN_HEADS * HEAD_DIM
CONV_WIDTH = 1024
CONV_K = 3
MOBA_BLOCK = 256
MOBA_TOPK = 3
ROPE_THETA = 500000.0
ROPE_DIM = HEAD_DIM // 4
N_XATTN_HEADS = 4
XATTN_WIDTH = N_XATTN_HEADS * HEAD_DIM
NORM_EPS = 1e-6

LOG2E = math.log2(math.e)
ATTN_SCALE = HEAD_DIM ** -0.5
NEG_INF = float("-inf")

VMEM_LIMIT_BYTES = 58 * 1024 * 1024
SUBLANES = 8
BF16_SUBLANES = 16
MXU_COLS = 256

NT_DIMS = (((1,), (1,)), ((), ()))


def _rms_norm_bf16(xf, g):
    ms = jnp.mean(xf * xf, axis=-1, keepdims=True)
    return (xf * lax.rsqrt(ms + NORM_EPS) * g).astype(BF16)


def _resident(shape):
    return pl.BlockSpec(shape, lambda *_: (0,) * len(shape), pipeline_mode=pl.Buffered(1))


def _cast_job(w, steps, n_cols=None, col_block=0, step_of=lambda i, *_: i):
    n_rows = w.shape[0]
    n_cols = w.shape[1] if n_cols is None else n_cols
    rows = n_rows // steps
    assert rows * steps == n_rows and rows % BF16_SUBLANES == 0
    in_spec = pl.BlockSpec((rows, n_cols), lambda *g: (step_of(*g), col_block))
    out_spec = pl.BlockSpec((rows, n_cols), lambda *g: (step_of(*g), 0))
    return w, in_spec, out_spec, jax.ShapeDtypeStruct((n_rows, n_cols), BF16)


def _call(body, name, grid, operands, in_specs, out_specs, out_shape, scratch_shapes=(), casts=(),
          prefetch=()):
    n_in = len(prefetch) + len(operands)
    n_out = len(out_shape)
    n_cast = len(casts)

    def kern(*refs):
        ins, refs = refs[:n_in], refs[n_in:]
        cast_src, refs = refs[:n_cast], refs[n_cast:]
        outs, refs = refs[:n_out], refs[n_out:]
        cast_dst, scratch = refs[:n_cast], refs[n_cast:]
        body(*ins, *outs, *scratch)
        for src, dst in zip(cast_src, cast_dst):
            dst[...] = src[...].astype(BF16)

    grid_spec = pltpu.PrefetchScalarGridSpec(
        num_scalar_prefetch=len(prefetch),
        grid=grid,
        in_specs=list(in_specs) + [c[1] for c in casts],
        out_specs=tuple(out_specs) + tuple(c[2] for c in casts),
        scratch_shapes=list(scratch_shapes),
    )
    outs = pl.pallas_call(
        kern,
        grid_spec=grid_spec,
        out_shape=tuple(out_shape) + tuple(c[3] for c in casts),
        compiler_params=pltpu.CompilerParams(
            dimension_semantics=("arbitrary",) * len(grid), vmem_limit_bytes=VMEM_LIMIT_BYTES),
        name=name,
    )(*prefetch, *operands, *[c[0] for c in casts])
    return outs[:n_out], outs[n_out:]


NORM_TM = 1024


def _norm_kernel(x_ref, g_ref, o_ref):
    o_ref[...] = _rms_norm_bf16(x_ref[...], g_ref[...])


def _norm(xf, g, casts):
    T, D = xf.shape
    tm = NORM_TM
    return _call(
        _norm_kernel, "norm_mix", (T // tm,), (xf, g),
        in_specs=[pl.BlockSpec((tm, D), lambda i: (i, 0)), pl.BlockSpec((1, D), lambda i: (0, 0))],
        out_specs=[pl.BlockSpec((tm, D), lambda i: (i, 0))],
        out_shape=[jax.ShapeDtypeStruct((T, D), BF16)],
        casts=casts)


PROJ_TM = 1024
VT_SPAN = 512


def _rope(t, cos, sin_lo, sin_hi):
    return (t * cos + pltpu.roll(t, HEAD_DIM - ROPE_DIM // 2, 1) * sin_lo
            + pltpu.roll(t, ROPE_DIM // 2, 1) * sin_hi)


def _qkv_kernel(h_ref, w_ref, cos_ref, slo_ref, shi_ref, q_ref, k_ref, vt_ref, km_ref):
    tm = PROJ_TM
    h = h_ref[...]
    cos, slo, shi = cos_ref[...], slo_ref[...], shi_ref[...]
    heads_per_chunk = MXU_COLS // HEAD_DIM
    for c in range(ATTN_WIDTH // MXU_COLS):
        csl = slice(c * MXU_COLS, (c + 1) * MXU_COLS)
        qc = jnp.dot(h, w_ref[:, csl], preferred_element_type=F32)
        kc = jnp.dot(h, w_ref[:, ATTN_WIDTH + c * MXU_COLS:ATTN_WIDTH + (c + 1) * MXU_COLS],
                     preferred_element_type=F32)
        vc = jnp.dot(h, w_ref[:, 2 * ATTN_WIDTH + c * MXU_COLS:2 * ATTN_WIDTH + (c + 1) * MXU_COLS],
                     preferred_element_type=F32)
        for r in range(tm // VT_SPAN):
            vt_ref[r, csl, :] = vc[r * VT_SPAN:(r + 1) * VT_SPAN].T.astype(BF16)
        for hh in range(heads_per_chunk):
            sl = slice(hh * HEAD_DIM, (hh + 1) * HEAD_DIM)
            osl = slice(c * MXU_COLS + hh * HEAD_DIM, c * MXU_COLS + (hh + 1) * HEAD_DIM)
            q_ref[:, osl] = (_rope(qc[:, sl], cos, slo, shi) * (ATTN_SCALE * LOG2E)).astype(BF16)
            kf = _rope(kc[:, sl], cos, slo, shi)
            k_ref[:, osl] = kf.astype(BF16)
            for r in range(tm // MOBA_BLOCK):
                km_ref[r, :, osl] = jnp.mean(kf[r * MOBA_BLOCK:(r + 1) * MOBA_BLOCK], axis=0, keepdims=True)


def _qkv_proj(h, w_qkv, cos, slo, shi, seq, casts):
    T, D = h.shape
    tm = PROJ_TM
    tpb = seq // tm
    row = lambda i: (i, 0)
    pos = lambda i: (i % tpb, 0)
    return _call(
        _qkv_kernel, "qkv_proj", (T // tm,), (h, w_qkv, cos, slo, shi),
        in_specs=[
            pl.BlockSpec((tm, D), row),
            _resident((D, 3 * ATTN_WIDTH)),
            pl.BlockSpec((tm, HEAD_DIM), pos),
            pl.BlockSpec((tm, HEAD_DIM), pos),
            pl.BlockSpec((tm, HEAD_DIM), pos),
        ],
        out_specs=[
            pl.BlockSpec((tm, ATTN_WIDTH), row),
            pl.BlockSpec((tm, ATTN_WIDTH), row),
            pl.BlockSpec((tm // VT_SPAN, ATTN_WIDTH, VT_SPAN), lambda i: (i, 0, 0)),
            pl.BlockSpec((tm // MOBA_BLOCK, 1, ATTN_WIDTH), lambda i: (i, 0, 0)),
        ],
        out_shape=[
            jax.ShapeDtypeStruct((T, ATTN_WIDTH), BF16),
            jax.ShapeDtypeStruct((T, ATTN_WIDTH), BF16),
            jax.ShapeDtypeStruct((T // VT_SPAN, ATTN_WIDTH, VT_SPAN), BF16),
            jax.ShapeDtypeStruct((T // MOBA_BLOCK, 1, ATTN_WIDTH), F32),
        ],
        casts=casts)


def _conv_kernel(tiles_per_batch, h_ref, w_ref, cw_ref, z_ref, u_scr):
    tm = PROJ_TM
    i = pl.program_id(0)

    @pl.when(i % tiles_per_batch == 0)
    def _():
        u_scr[0:SUBLANES, :] = jnp.zeros((SUBLANES, CONV_WIDTH), F32)

    h = h_ref[...]
    for c in range(CONV_WIDTH // MXU_COLS):
        csl = slice(c * MXU_COLS, (c + 1) * MXU_COLS)
        cx = jnp.dot(h, w_ref[:, csl], preferred_element_type=F32)
        cb = jnp.dot(h, w_ref[:, CONV_WIDTH + c * MXU_COLS:CONV_WIDTH + (c + 1) * MXU_COLS],
                     preferred_element_type=F32)
        cc = jnp.dot(h, w_ref[:, 2 * CONV_WIDTH + c * MXU_COLS:2 * CONV_WIDTH + (c + 1) * MXU_COLS],
                     preferred_element_type=F32)
        u = cc * cx
        u_scr[SUBLANES:tm + SUBLANES, csl] = u
        conv = (cw_ref[0:1, csl] * u_scr[SUBLANES - 2:tm + SUBLANES - 2, csl]
                + cw_ref[1:2, csl] * u_scr[SUBLANES - 1:tm + SUBLANES - 1, csl]
                + cw_ref[2:3, csl] * u)
        z_ref[:, csl] = (cb * conv).astype(BF16)
        u_scr[0:SUBLANES, csl] = u[tm - SUBLANES:tm]


def _conv_proj(h, w_conv, conv_w, seq, casts):
    T, D = h.shape
    tm = PROJ_TM
    return _call(
        functools.partial(_conv_kernel, seq // tm), "conv_proj", (T // tm,), (h, w_conv, conv_w),
        in_specs=[
            pl.BlockSpec((tm, D), lambda i: (i, 0)),
            _resident((D, 3 * CONV_WIDTH)),
            pl.BlockSpec((CONV_K, CONV_WIDTH), lambda i: (0, 0)),
        ],
        out_specs=[pl.BlockSpec((tm, CONV_WIDTH), lambda i: (i, 0))],
        out_shape=[jax.ShapeDtypeStruct((T, CONV_WIDTH), BF16)],
        scratch_shapes=[pltpu.VMEM((tm + SUBLANES, CONV_WIDTH), F32)],
        casts=casts)


def _gate_kernel(h_ref, wa_ref, wc_ref, o_ref):
    h = h_ref[...]
    for g, w_ref in enumerate((wa_ref, wc_ref)):
        n = w_ref.shape[1]
        for c in range(n // MXU_COLS):
            csl = slice(c * MXU_COLS, (c + 1) * MXU_COLS)
            o_ref[:, g * n + c * MXU_COLS:g * n + (c + 1) * MXU_COLS] = jax.nn.sigmoid(
                jnp.dot(h, w_ref[:, csl], preferred_element_type=F32)).astype(BF16)


def _gate_proj(h, w_ga, w_gc, casts):
    T, D = h.shape
    tm = PROJ_TM
    return _call(
        _gate_kernel, "gate_proj", (T // tm,), (h, w_ga, w_gc),
        in_specs=[pl.BlockSpec((tm, D), lambda i: (i, 0)), _resident((D, D)), _resident((D, D))],
        out_specs=[pl.BlockSpec((tm, 2 * D), lambda i: (i, 0))],
        out_shape=[jax.ShapeDtypeStruct((T, 2 * D), BF16)],
        casts=casts)


MOBA_PAIR = 2
MOBA_UNROLL = 32
MOBA_AHEAD = 16
MASK_NEG = -1e30
ONES_ROWS = BF16_SUBLANES


def _moba_items(nb):
    qbs, prs = [], []
    for qb in range(nb):
        for pr in range((qb + MOBA_PAIR - 1) // MOBA_PAIR):
            qbs.append(qb)
            prs.append(pr)
    assert len(qbs) % MOBA_UNROLL == 0
    return np.asarray(qbs, np.int32), np.asarray(prs, np.int32), len(qbs) // MOBA_UNROLL


def _moba_kernel(nb, n_trips, tqb_ref, tpr_ref, q_ref, k_ref, vt_ref, km_ref, o_ref,
                 qbias_scr, s_scr, smax_scr, acc_scr, m_scr):
    blk = MOBA_BLOCK
    span = MOBA_PAIR * blk
    vt_blocks = vt_ref.shape[2] // blk
    vt_tiles = MOBA_PAIR // vt_blocks
    ones = jnp.ones((ONES_ROWS, span), BF16)
    lane_row = lax.broadcasted_iota(jnp.int32, (1, HEAD_DIM), 1)

    km = km_ref[...]
    km_hi = km.astype(BF16)
    km_lo = (km - km_hi.astype(F32)).astype(BF16)
    bidx = lax.broadcasted_iota(jnp.int32, (nb, blk), 0)
    causal = (lax.broadcasted_iota(jnp.int32, (blk, blk), 0)
              <= lax.broadcasted_iota(jnp.int32, (blk, blk), 1))

    def choose_and_score(qb):
        rows = slice(qb * blk, (qb + 1) * blk)
        q = q_ref[rows, :]
        gate = (lax.dot_general(km_hi, q, NT_DIMS, preferred_element_type=F32)
                + lax.dot_general(km_lo, q, NT_DIMS, preferred_element_type=F32))
        g = jnp.where(bidx < qb, gate, NEG_INF)
        bias = jnp.full((nb, blk), MASK_NEG, F32)
        for _ in range(MOBA_TOPK):
            top = jnp.max(g, axis=0, keepdims=True)
            first = jnp.min(jnp.where(g == top, bidx, nb), axis=0, keepdims=True)
            hit = bidx == first
            bias = jnp.where(hit & (top > NEG_INF), 0.0, bias)
            g = jnp.where(hit, NEG_INF, g)
        bias = jnp.concatenate([bias, jnp.zeros((HEAD_DIM - nb, blk), F32)], axis=0)
        qbias_scr[rows, :] = bias.T.astype(BF16)
        return lax.dot_general(k_ref[rows, :], q, NT_DIMS, preferred_element_type=F32)

    def own_block(qb, s):
        s = jnp.where(causal, s, NEG_INF)
        m = jnp.max(s, axis=0, keepdims=True)
        p = jnp.exp2(s - m).astype(BF16)
        col = (qb % vt_blocks) * blk
        vt = jnp.concatenate([vt_ref[qb // vt_blocks, :, col:col + blk], ones[:, :blk]], axis=0)
        acc_scr[qb] = jnp.dot(vt, p, preferred_element_type=F32)
        m_scr[qb] = m

    def scores(idx, slot):
        qb, pr = tqb_ref[idx], tpr_ref[idx]
        qrow = pl.multiple_of(qb * blk, blk)
        krow = pl.multiple_of(pr * span, span)
        hot = jnp.concatenate(
            [jnp.broadcast_to(jnp.where(lane_row == pr * MOBA_PAIR + u, 1.0, 0.0).astype(BF16), (blk, HEAD_DIM))
             for u in range(MOBA_PAIR)], axis=0)
        k_ext = jnp.concatenate([k_ref[pl.ds(krow, span), :], hot], axis=1)
        q_ext = jnp.concatenate([q_ref[pl.ds(qrow, blk), :], qbias_scr[pl.ds(qrow, blk), :]], axis=1)
        s = lax.dot_general(k_ext, q_ext, NT_DIMS, preferred_element_type=F32)
        s_scr[slot] = s
        smax_scr[slot] = jnp.max(s.reshape(span // SUBLANES, SUBLANES, blk), axis=0)

    def softmax_pv(idx, slot, m):
        qb, pr = tqb_ref[idx], tpr_ref[idx]
        m = jnp.where(pr == 0, m_scr[qb], m)
        m_new = jnp.maximum(m, jnp.max(smax_scr[slot], axis=0, keepdims=True))
        p = jnp.exp2(s_scr[slot] - m_new).astype(BF16)
        vt = jnp.concatenate([vt_ref[pr * vt_tiles + j] for j in range(vt_tiles)], axis=1)
        vt = jnp.concatenate([vt, ones], axis=0)
        acc_scr[qb] = jnp.exp2(m - m_new) * acc_scr[qb] + jnp.dot(vt, p, preferred_element_type=F32)
        return m_new

    def step(base, i, m, last=False):
        m = softmax_pv(base + i, i, m)
        if not last or MOBA_AHEAD + i < MOBA_UNROLL:
            scores(base + MOBA_AHEAD + i, (MOBA_AHEAD + i) % MOBA_UNROLL)
        return m

    def trip(t, m):
        for i in range(MOBA_UNROLL):
            m = step(t * MOBA_UNROLL, i, m)
        return m

    own_lag = 2
    scored = [choose_and_score(qb) for qb in range(min(own_lag, nb))]
    for qb in range(nb):
        if qb + own_lag < nb:
            scored.append(choose_and_score(qb + own_lag))
        own_block(qb, scored[qb])
    for i in range(MOBA_AHEAD):
        scores(i, i)

    m = lax.fori_loop(0, n_trips - 1, trip, jnp.zeros((1, blk), F32))
    for i in range(MOBA_UNROLL):
        m = step((n_trips - 1) * MOBA_UNROLL, i, m, last=True)

    for qb in range(nb):
        acc = acc_scr[qb]
        inv = 1.0 / acc[HEAD_DIM:HEAD_DIM + 1]
        o_ref[qb * blk:(qb + 1) * blk, :] = (acc[:HEAD_DIM] * inv).T.astype(BF16)


def _moba_attention(q, k, vt, kmean, batch, seq, casts):
    T = q.shape[0]
    blk = MOBA_BLOCK
    nb = seq // blk
    span = MOBA_PAIR * blk
    vt_span = vt.shape[2]
    assert nb % MOBA_PAIR == 0 and nb <= HEAD_DIM and span % vt_span == 0
    assert vt.shape == (T // vt_span, ATTN_WIDTH, vt_span)
    tqb, tpr, n_trips = _moba_items(nb)
    head = lambda b, h, *_: (b, h)
    return _call(
        functools.partial(_moba_kernel, nb, n_trips), "moba_attention", (batch, N_ATTN_HEADS),
        (q, k, vt, kmean),
        in_specs=[
            pl.BlockSpec((seq, HEAD_DIM), head),
            pl.BlockSpec((seq, HEAD_DIM), head),
            pl.BlockSpec((seq // vt_span, HEAD_DIM, vt_span), lambda b, h, *_: (b, h, 0)),
            pl.BlockSpec((nb, HEAD_DIM), head),
        ],
        out_specs=[pl.BlockSpec((seq, HEAD_DIM), head)],
        out_shape=[jax.ShapeDtypeStruct((T, ATTN_WIDTH), BF16)],
        scratch_shapes=[
            pltpu.VMEM((seq, HEAD_DIM), BF16),
            pltpu.VMEM((MOBA_UNROLL, span, blk), F32),
            pltpu.VMEM((MOBA_UNROLL, SUBLANES, blk), F32),
            pltpu.VMEM((nb, HEAD_DIM + ONES_ROWS, blk), F32),
            pltpu.VMEM((nb, 1, blk), F32),
        ],
        casts=casts,
        prefetch=(jnp.asarray(tqb), jnp.asarray(tpr)))


MIX_TM = 512


def _mix_kernel(a_ref, z_ref, sga_ref, sgc_ref, x_ref, wa_ref, wc_ref, wm_ref, o_ref):
    y_attn = jnp.dot(a_ref[...], wa_ref[...], preferred_element_type=F32)
    y_conv = jnp.dot(z_ref[...], wc_ref[...], preferred_element_type=F32)
    merged = sga_ref[...].astype(F32) * y_attn + sgc_ref[...].astype(F32) * y_conv
    o_ref[...] = x_ref[...] + jnp.dot(merged.astype(BF16), wm_ref[...], preferred_element_type=F32)


def _mix(attn, z, gates, xf, w_attn_out, w_conv_out, w_mix_out, casts):
    T, D = xf.shape
    tm = MIX_TM
    row = lambda i: (i, 0)
    return _call(
        _mix_kernel, "mix", (T // tm,), (attn, z, gates, gates, xf, w_attn_out, w_conv_out, w_mix_out),
        in_specs=[
            pl.BlockSpec((tm, ATTN_WIDTH), row),
            pl.BlockSpec((tm, CONV_WIDTH), row),
            pl.BlockSpec((tm, D), row),
            pl.BlockSpec((tm, D), lambda i: (i, 1)),
            pl.BlockSpec((tm, D), row),
            _resident((ATTN_WIDTH, D)),
            _resident((CONV_WIDTH, D)),
            _resident((D, D)),
        ],
        out_specs=[pl.BlockSpec((tm, D), row)],
        out_shape=[jax.ShapeDtypeStruct((T, D), F32)],
        casts=casts)


def _mem_kv_kernel(mem_ref, g_ref, w_ref, o_ref):
    o_ref[...] = jnp.dot(_rms_norm_bf16(mem_ref[...], g_ref[...]), w_ref[...],
                         preferred_element_type=F32).astype(BF16)


def _mem_kv(memf, g, wkv):
    R, D = memf.shape
    N = wkv.shape[1]
    return _call(
        _mem_kv_kernel, "mem_kv", (1,), (memf, g, wkv),
        in_specs=[pl.BlockSpec((R, D), lambda i: (0, 0)),
                  pl.BlockSpec((1, D), lambda i: (0, 0)),
                  pl.BlockSpec((D, N), lambda i: (0, 0))],
        out_specs=[pl.BlockSpec((R, N), lambda i: (0, 0))],
        out_shape=[jax.ShapeDtypeStruct((R, N), BF16)])


XATTN_TM = 1024


def _xattn_kernel(x_ref, g_ref, wq_ref, kv_ref, wo_ref, o_ref):
    xf = x_ref[...]
    hq = jnp.dot(_rms_norm_bf16(xf, g_ref[...]), wq_ref[...], preferred_element_type=F32)
    hq = (hq * (ATTN_SCALE * LOG2E)).astype(BF16)
    outs = []
    for hh in range(N_XATTN_HEADS):
        sl = slice(hh * HEAD_DIM, (hh + 1) * HEAD_DIM)
        mk = kv_ref[:, sl]
        mv = kv_ref[:, XATTN_WIDTH + hh * HEAD_DIM:XATTN_WIDTH + (hh + 1) * HEAD_DIM]
        s = lax.dot_general(hq[:, sl], mk, NT_DIMS, preferred_element_type=F32)
        p = jnp.exp2(s - jnp.max(s, axis=-1, keepdims=True))
        inv = 1.0 / jnp.sum(p, axis=-1, keepdims=True)
        outs.append((jnp.dot(p.astype(BF16), mv, preferred_element_type=F32) * inv).astype(BF16))
    att = jnp.concatenate(outs, axis=-1)
    o_ref[...] = xf + jnp.dot(att, wo_ref[...], preferred_element_type=F32)


def _xattn(x1, g, wq, kv, wo, seq, n_mem):
    T, D = x1.shape
    tm = XATTN_TM
    tpb = seq // tm
    return _call(
        _xattn_kernel, "xattn", (T // tm,), (x1, g, wq, kv, wo),
        in_specs=[
            pl.BlockSpec((tm, D), lambda i: (i, 0)),
            pl.BlockSpec((1, D), lambda i: (0, 0)),
            _resident((D, XATTN_WIDTH)),
            pl.BlockSpec((n_mem, 2 * XATTN_WIDTH), lambda i: (i // tpb, 0)),
            _resident((XATTN_WIDTH, D)),
        ],
        out_specs=[pl.BlockSpec((tm, D), lambda i: (i, 0))],
        out_shape=[jax.ShapeDtypeStruct((T, D), F32)])


MLP_TM = 512
MLP_TF = 1024


def _mlp_kernel(x_ref, g_ref, wu_ref, wd_ref, gf_ref, o_ref, h_scr, acc_scr):
    f = pl.program_id(1)

    @pl.when(f == 0)
    def _():
        xf = x_ref[...]
        h_scr[...] = _rms_norm_bf16(xf, g_ref[...])
        acc_scr[...] = xf

    u = jnp.maximum(jnp.dot(h_scr[...], wu_ref[...], preferred_element_type=F32), 0.0)
    acc_scr[...] += jnp.dot((u * u).astype(BF16), wd_ref[...], preferred_element_type=F32)

    @pl.when(f == pl.num_programs(1) - 1)
    def _():
        y = acc_scr[...]
        ms = jnp.mean(y * y, axis=-1, keepdims=True)
        o_ref[...] = y * lax.rsqrt(ms + NORM_EPS) * gf_ref[...]


def _mlp(x2, g, w_up, w_down, g_final):
    T, D = x2.shape
    FF = w_up.shape[1]
    tm, tf = MLP_TM, MLP_TF
    return _call(
        _mlp_kernel, "mlp", (T // tm, FF // tf), (x2, g, w_up, w_down, g_final),
        in_specs=[
            pl.BlockSpec((tm, D), lambda i, f: (i, 0)),
            pl.BlockSpec((1, D), lambda i, f: (0, 0)),
            pl.BlockSpec((D, tf), lambda i, f: (0, f)),
            pl.BlockSpec((tf, D), lambda i, f: (f, 0)),
            pl.BlockSpec((1, D), lambda i, f: (0, 0)),
        ],
        out_specs=[pl.BlockSpec((tm, D), lambda i, f: (i, 0))],
        out_shape=[jax.ShapeDtypeStruct((T, D), F32)],
        scratch_shapes=[pltpu.VMEM((tm, D), BF16), pltpu.VMEM((tm, D), F32)])


def _rope_tables(seq):
    half = ROPE_DIM // 2
    inv_freq = ROPE_THETA ** (-np.arange(half, dtype=np.float64) / half)
    ang = np.arange(seq, dtype=np.float64)[:, None] * inv_freq[None, :]
    cos, sin = np.cos(ang), np.sin(ang)
    pad = HEAD_DIM - ROPE_DIM
    cos_t = np.concatenate([cos, cos, np.ones((seq, pad))], axis=-1)
    sin_lo = np.concatenate([-sin, np.zeros((seq, half + pad))], axis=-1)
    sin_hi = np.concatenate([np.zeros((seq, half)), sin, np.zeros((seq, pad))], axis=-1)
    return tuple(jnp.asarray(t, F32) for t in (cos_t, sin_lo, sin_hi))


def _layer(xf, memf, batch, seq, n_mem, norm_mix, w_in, conv_w, w_attn_out, w_conv_out, w_mix_out,
           norm_xattn, norm_mem, wq_x, wkv_x, wo_x, norm_mlp, w_up, w_down, g_out):
    T, D = xf.shape
    row = lambda g: g.reshape(1, D).astype(F32)
    cos_t, sin_lo, sin_hi = _rope_tables(seq)
    n_qkv, n_conv = 3 * ATTN_WIDTH, 3 * CONV_WIDTH
    assert n_qkv == n_conv and (n_qkv + n_conv) % D == 0
    proj_steps = T // PROJ_TM
    head_step = lambda b, h, *_: b * N_ATTN_HEADS + h

    (h,), (w_qkv,) = _norm(xf, row(norm_mix), [_cast_job(w_in, T // NORM_TM, n_qkv, 0)])
    (q, k, vt, kmean), (w_conv, w_ga, w_gc) = _qkv_proj(
        h, w_qkv, cos_t, sin_lo, sin_hi, seq,
        [_cast_job(w_in, proj_steps, n_conv, 1),
         _cast_job(w_in, proj_steps, D, (n_qkv + n_conv) // D),
         _cast_job(w_in, proj_steps, D, (n_qkv + n_conv) // D + 1)])
    (z,), (wa, wc, wm) = _conv_proj(
        h, w_conv, conv_w, seq,
        [_cast_job(w_attn_out, proj_steps), _cast_job(w_conv_out, proj_steps), _cast_job(w_mix_out, proj_steps)])
    (gates,), (wq, wkv, wo) = _gate_proj(
        h, w_ga, w_gc, [_cast_job(wq_x, proj_steps), _cast_job(wkv_x, proj_steps), _cast_job(wo_x, proj_steps)])
    kmean = kmean.reshape(T // MOBA_BLOCK, ATTN_WIDTH)
    (attn,), (wu,) = _moba_attention(
        q, k, vt, kmean, batch, seq, [_cast_job(w_up, batch * N_ATTN_HEADS, step_of=head_step)])
    (x1,), (wd,) = _mix(attn, z, gates, xf, wa, wc, wm, [_cast_job(w_down, T // MIX_TM)])
    (kv,), _ = _mem_kv(memf, row(norm_mem), wkv)
    (x2,), _ = _xattn(x1, row(norm_xattn), wq, kv, wo, seq, n_mem)
    (out,), _ = _mlp(x2, row(norm_mlp), wu, wd, g_out)
    return out


def kernel(x, mem, norm_mix, w_in, conv_w, w_attn_out, w_conv_out, w_mix_out, norm_xattn, norm_mem,
           wq_x, wkv_x, wo_x, norm_mlp, w_up, w_down, norm_final):
    B, S, D = x.shape
    n_mem = mem.shape[1]
    depth = w_in.shape[0]
    assert depth == 1, "the final rmsnorm is fused into the last layer's MLP kernel"
    xf = x.reshape(B * S, D)
    memf = mem.reshape(B * n_mem, D)
    out = _layer(xf, memf, B, S, n_mem, norm_mix[0], w_in[0], conv_w[0], w_attn_out[0], w_conv_out[0],
                 w_mix_out[0], norm_xattn[0], norm_mem[0], wq_x[0], wkv_x[0], wo_x[0], norm_mlp[0],
                 w_up[0], w_down[0], norm_final.reshape(1, D).astype(F32))
    return out.reshape(B, S, D)
```
